```python
import jax
import jax.numpy as jnp
from jax import lax
import numpy as np


D_MODEL = 1024
BATCH = 2
SEQ = 8192
DEPTH = 4

N_MIXERS = 3
N_MEM = 256
CHUNK = 64
GDN_DK = 128
GDN_DV = 128
GDN_HEADS = D_MODEL // GDN_DV
GDN_QK = GDN_HEADS * GDN_DK
GDN_V = GDN_HEADS * GDN_DV
CONV_K = 4
MLSTM_HEADS = 8
MLSTM_DV = D_MODEL // MLSTM_HEADS
MLSTM_DQK = MLSTM_DV // 2
MLSTM_QK = MLSTM_HEADS * MLSTM_DQK
MLSTM_V = MLSTM_HEADS * MLSTM_DV
FOX_DH = 64
FOX_HEADS = D_MODEL // FOX_DH
FOX_W = FOX_HEADS * FOX_DH
FOX_BLOCK = 128
MEM_HEADS = 4
MEM_DH = D_MODEL // 8
MEM_W = MEM_HEADS * MEM_DH
MIX_W = D_MODEL
D_FF = 4 * D_MODEL
ALPHA = (2 * DEPTH) ** 0.25
BETA_INIT = (8 * DEPTH) ** -0.25
LN_EPS = 1e-5
NORM_EPS = 1e-6

kernel_name = 'hybrid_gdn_mlstm_fox_memory_deepnorm'


def split_cols(t, sizes):
    return jnp.split(t, np.cumsum(sizes)[:-1].tolist(), axis=-1)


def layer_norm(x, g, b):
    xf = x.astype(jnp.float32)
    mu = jnp.mean(xf, -1, keepdims=True)
    var = jnp.mean(jnp.square(xf - mu), -1, keepdims=True)
    return ((xf - mu) * lax.rsqrt(var + LN_EPS)).astype(x.dtype) * g + b


def rms_norm(x, g):
    xf = x.astype(jnp.float32)
    return xf * lax.rsqrt(jnp.mean(jnp.square(xf), -1, keepdims=True) + NORM_EPS) * g


def head_layer_norm(x, g):
    mu = jnp.mean(x, -1, keepdims=True)
    var = jnp.mean(jnp.square(x - mu), -1, keepdims=True)
    return (x - mu) * lax.rsqrt(var + NORM_EPS) * g


def l2_normalize(x):
    return x * lax.rsqrt(jnp.sum(jnp.square(x), -1, keepdims=True) + NORM_EPS)


def causal_dwconv(x, w):
    k, c = w.shape
    return lax.conv_general_dilated(x, w[:, None, :].astype(x.dtype), window_strides=(1,),
                                    padding=[(k - 1, 0)], dimension_numbers=('NWC', 'WIO', 'NWC'),
                                    feature_group_count=c)


def to_chunks(t):
    bsz, seq, h = t.shape[:3]
    t = t.reshape(bsz, seq // CHUNK, CHUNK, h, *t.shape[3:])
    return jnp.moveaxis(t, 3, 1)


def from_chunks(t):
    n, bsz, h, c, d = t.shape
    return jnp.transpose(t, (1, 0, 3, 2, 4)).reshape(bsz, n * c, h, d)


def gated_delta_rule(q, k, v, g, beta):
    dk = q.shape[-1]
    dv = v.shape[-1]
    q = to_chunks(q) * dk ** -0.5
    k = to_chunks(k)
    v = to_chunks(v)
    g = to_chunks(g)
    beta = to_chunks(beta)
    gc = jnp.cumsum(g, -1)
    idx = jnp.arange(CHUNK)
    incl = idx[:, None] >= idx[None, :]
    strict = idx[:, None] > idx[None, :]
    decay = jnp.exp(jnp.where(incl, gc[..., :, None] - gc[..., None, :], -jnp.inf))
    kb = k * beta[..., None]
    a_mat = jnp.where(strict, jnp.einsum('bhncd,bhnsd->bhncs', kb, k) * decay, 0.0)
    rhs = jnp.concatenate([v * beta[..., None], kb * jnp.exp(gc)[..., None]], -1)
    sol = lax.linalg.triangular_solve(a_mat, rhs, left_side=True, lower=True, unit_diagonal=True)
    u, w = sol[..., :dv], sol[..., dv:]
    attn = jnp.einsum('bhncd,bhnsd->bhncs', q, k) * decay
    qg = q * jnp.exp(gc)[..., None]
    g_last = gc[..., -1:]
    kg = k * jnp.exp(g_last - gc)[..., None]
    d_last = jnp.exp(g_last[..., 0])
    xs = tuple(jnp.moveaxis(t, 2, 0) for t in (u, w, qg, kg, attn, d_last))
    bsz, h = q.shape[:2]
    s0 = jnp.zeros((bsz, h, dk, dv), jnp.float32)

    def step(state, inp):
        u_c, w_c, qg_c, kg_c, a_c, dl_c = inp
        v_new = u_c - jnp.einsum('bhcd,bhde->bhce', w_c, state)
        o_c = jnp.einsum('bhcd,bhde->bhce', qg_c, state) + jnp.einsum('bhcs,bhse->bhce', a_c, v_new)
        state = state * dl_c[..., None, None] + jnp.einsum('bhcd,bhce->bhde', kg_c, v_new)
        return state, o_c

    _, o = lax.scan(step, s0, xs)
    return from_chunks(o)


def mlstm_chunkwise(q, k, v, i_log, f_log):
    dqk = q.shape[-1]
    dv = v.shape[-1]
    q = to_chunks(q) * dqk ** -0.5
    k = to_chunks(k)
    v = to_chunks(v)
    i_log = to_chunks(i_log)
    f_log = to_chunks(f_log)
    b = jnp.cumsum(f_log, -1)
    idx = jnp.arange(CHUNK)
    incl = idx[:, None] >= idx[None, :]
    log_d = jnp.where(incl, b[..., :, None] - b[..., None, :] + i_log[..., None, :], -jnp.inf)
    m_intra = jnp.max(log_d, -1)
    qk = jnp.einsum('bhncd,bhnsd->bhncs', q, k)
    b_last = b[..., -1]
    log_w = b_last[..., None] - b + i_log
    m_w = jnp.max(log_w, -1)
    xs = tuple(jnp.moveaxis(t, 2, 0) for t in (q, k, v, b, log_d, m_intra, qk, b_last, log_w, m_w))
    bsz, h = q.shape[:2]
    init = (jnp.zeros((bsz, h, dqk, dv), jnp.float32), jnp.zeros((bsz, h, dqk), jnp.float32),
            jnp.zeros((bsz, h), jnp.float32))

    def step(carry, inp):
        c_st, n_st, m_prev = carry
        q_c, k_c, v_c, b_c, logd_c, mi_c, qk_c, bl_c, lw_c, mw_c = inp
        a_inter = b_c + m_prev[..., None]
        m_t = jnp.maximum(a_inter, mi_c)
        w_inter = jnp.exp(a_inter - m_t)
        p = qk_c * jnp.exp(logd_c - m_t[..., None])
        num = w_inter[..., None] * jnp.einsum('bhcd,bhde->bhce', q_c, c_st) + jnp.einsum('bhcs,bhse->bhce', p, v_c)
        den = w_inter * jnp.einsum('bhcd,bhd->bhc', q_c, n_st) + jnp.sum(p, -1)
        h_c = num / jnp.maximum(jnp.abs(den), jnp.exp(-m_t))[..., None]
        m_new = jnp.maximum(bl_c + m_prev, mw_c)
        keep = jnp.exp(bl_c + m_prev - m_new)
        kw = k_c * jnp.exp(lw_c - m_new[..., None])[..., None]
        c_st = keep[..., None, None] * c_st + jnp.einsum('bhcd,bhce->bhde', kw, v_c)
        n_st = keep[..., None] * n_st + jnp.sum(kw, -2)
        return (c_st, n_st, m_new), h_c

    _, hs = lax.scan(step, init, xs)
    return from_chunks(hs)


def forgetting_attention(q, k, v, f_log):
    bsz, seq, h, d = q.shape
    c = jnp.moveaxis(jnp.cumsum(f_log, axis=1), 1, 2)
    scale = d ** -0.5
    outs = []
    for blk in range(seq // FOX_BLOCK):
        lo, hi = blk * FOX_BLOCK, (blk + 1) * FOX_BLOCK
        logits = jnp.einsum('bqhd,bkhd->bhqk', q[:, lo:hi], k[:, :hi]) * scale
        logits = logits + c[:, :, lo:hi, None] - c[:, :, None, :hi]
        causal = (lo + jnp.arange(FOX_BLOCK))[:, None] >= jnp.arange(hi)[None, :]
        p = jax.nn.softmax(jnp.where(causal, logits, -jnp.inf), axis=-1)
        outs.append(jnp.einsum('bhqk,bkhd->bqhd', p, v[:, :hi]))
    return jnp.concatenate(outs, axis=1)


def gdn_mixer(x, w_in, conv_w, a_log, dt_bias, norm_g):
    bsz, seq, _ = x.shape
    proj = x @ w_in
    qkv, z, a, bg, memq = split_cols(proj, (2 * GDN_QK + GDN_V, GDN_V, GDN_HEADS, GDN_HEADS, MEM_W))
    qkv = jax.nn.silu(causal_dwconv(qkv, conv_w)).astype(jnp.float32)
    q, k, v = split_cols(qkv, (GDN_QK, GDN_QK, GDN_V))
    q = l2_normalize(q.reshape(bsz, seq, GDN_HEADS, GDN_DK))
    k = l2_normalize(k.reshape(bsz, seq, GDN_HEADS, GDN_DK))
    v = v.reshape(bsz, seq, GDN_HEADS, GDN_DV)
    g = -jnp.exp(a_log.astype(jnp.float32)) * jax.nn.softplus((a + dt_bias).astype(jnp.float32))
    beta = jax.nn.sigmoid(bg.astype(jnp.float32))
    o = gated_delta_rule(q, k, v, g, beta)
    o = rms_norm(o, norm_g) * jax.nn.silu(z.reshape(bsz, seq, GDN_HEADS, GDN_DV).astype(jnp.float32))
    return o.reshape(bsz, seq, GDN_V).astype(x.dtype), memq


def mlstm_mixer(x, w_in, b_gate, norm_g):
    bsz, seq, _ = x.shape
    proj = x @ w_in
    q, k, v, og, ig, fg, memq = split_cols(
        proj, (MLSTM_QK, MLSTM_QK, MLSTM_V, MLSTM_V, MLSTM_HEADS, MLSTM_HEADS, MEM_W))
    q = q.reshape(bsz, seq, MLSTM_HEADS, MLSTM_DQK).astype(jnp.float32)
    k = k.reshape(bsz, seq, MLSTM_HEADS, MLSTM_DQK).astype(jnp.float32)
    v = v.reshape(bsz, seq, MLSTM_HEADS, MLSTM_DV).astype(jnp.float32)
    i_log = (ig + b_gate[0]).astype(jnp.float32)
    f_log = jax.nn.log_sigmoid((fg + b_gate[1]).astype(jnp.float32))
    h = mlstm_chunkwise(q, k, v, i_log, f_log)
    h = head_layer_norm(h, norm_g) * jax.nn.sigmoid(og.reshape(bsz, seq, MLSTM_HEADS, MLSTM_DV).astype(jnp.float32))
    return h.reshape(bsz, seq, MLSTM_V).astype(x.dtype), memq


def fox_mixer(x, w_in, b_f, qk_g):
    bsz, seq, _ = x.shape
    proj = x @ w_in
    q, k, v, og, fg, memq = split_cols(proj, (FOX_W, FOX_W, FOX_W, FOX_W, FOX_HEADS, MEM_W))

    def heads(t):
        return t.reshape(bsz, seq, FOX_HEADS, FOX_DH).astype(jnp.float32)

    q = rms_norm(heads(q), qk_g[0])
    k = rms_norm(heads(k), qk_g[1])
    f_log = jax.nn.log_sigmoid((fg + b_f).astype(jnp.float32))
    o = forgetting_attention(q, k, heads(v), f_log) * jax.nn.sigmoid(heads(og))
    return o.reshape(bsz, seq, FOX_W).astype(x.dtype), memq


def memory_attention(memq, mem, w_kv):
    bsz, seq, _ = memq.shape
    kv = mem @ w_kv
    k, v = split_cols(kv, (MEM_W, MEM_W))
    k = k.reshape(bsz, -1, MEM_HEADS, MEM_DH)
    v = v.reshape(bsz, -1, MEM_HEADS, MEM_DH)
    q = memq.reshape(bsz, seq, MEM_HEADS, MEM_DH)
    logits = jnp.einsum('bshd,bmhd->bhsm', q, k).astype(jnp.float32) * MEM_DH ** -0.5
    p = jax.nn.softmax(logits, axis=-1).astype(v.dtype)
    return jnp.einsum('bhsm,bmhd->bshd', p, v).reshape(bsz, seq, MEM_W)


def _normal(key, shape, scale):
    return jax.random.normal(key, shape, jnp.float32) * scale


def setup_inputs(seed: int = 0) -> dict:
    key = jax.random.key(seed)
    ks = jax.random.split(key, 24)
    n_a, n_b, n_c = ((DEPTH + 2 - kind) // N_MIXERS for kind in range(N_MIXERS))
    gdn_cols = 2 * GDN_QK + 2 * GDN_V + 2 * GDN_HEADS + MEM_W
    mlstm_cols = 2 * MLSTM_QK + 2 * MLSTM_V + 2 * MLSTM_HEADS + MEM_W
    fox_cols = 4 * FOX_W + FOX_HEADS + MEM_W
    dt = jnp.exp(jax.random.uniform(ks[4], (n_a, GDN_HEADS), jnp.float32, np.log(1e-3), np.log(1e-1)))
    f_bias = jnp.linspace(3.0, 6.0, MLSTM_HEADS, dtype=jnp.float32) + _normal(ks[8], (n_b, MLSTM_HEADS), 0.1)
    i_bias = _normal(ks[7], (n_b, MLSTM_HEADS), 0.1)
    return {
        'x': _normal(ks[0], (BATCH, SEQ, D_MODEL), 1.0),
        'mem': _normal(ks[1], (BATCH, N_MEM, D_MODEL), 1.0),
        'gdn_w_in': _normal(ks[2], (n_a, D_MODEL, gdn_cols), D_MODEL ** -0.5),
        'gdn_conv_w': _normal(ks[3], (n_a, CONV_K, 2 * GDN_QK + GDN_V), CONV_K ** -0.5),
        'gdn_a_log': jnp.log(jax.random.uniform(ks[5], (n_a, GDN_HEADS), jnp.float32, 1.0, 16.0)),
        'gdn_dt_bias': dt + jnp.log(-jnp.expm1(-dt)),
        'gdn_norm_g': 1.0 + _normal(ks[6], (n_a, GDN_DV), 0.02),
        'mlstm_w_in': _normal(ks[9], (n_b, D_MODEL, mlstm_cols), D_MODEL ** -0.5),
        'mlstm_b_gate': jnp.stack([i_bias, f_bias], axis=1),
        'mlstm_norm_g': 1.0 + _normal(ks[10], (n_b, MLSTM_HEADS, MLSTM_DV), 0.02),
        'fox_w_in': _normal(ks[11], (n_c, D_MODEL, fox_cols), D_MODEL ** -0.5),
        'fox_b_f': jax.random.uniform(ks[12], (n_c, FOX_HEADS), jnp.float32, 1.0, 4.0),
        'fox_qk_g': 1.0 + _normal(ks[13], (n_c, 2, FOX_DH), 0.02),
        'mem_w_kv': _normal(ks[14], (DEPTH, D_MODEL, 2 * MEM_W), D_MODEL ** -0.5),
        'w_out': _normal(ks[15], (DEPTH, MIX_W + MEM_W, D_MODEL), (MIX_W + MEM_W) ** -0.5 * BETA_INIT),
        'ln1_g': 1.0 + _normal(ks[16], (DEPTH, D_MODEL), 0.02),
        'ln1_b': _normal(ks[17], (DEPTH, D_MODEL), 0.02),
        'w_up': _normal(ks[18], (DEPTH, D_MODEL, D_FF), D_MODEL ** -0.5),
        'w_down': _normal(ks[19], (DEPTH, D_FF, D_MODEL), D_FF ** -0.5 * BETA_INIT),
        'ln2_g': 1.0 + _normal(ks[20], (DEPTH, D_MODEL), 0.02),
        'ln2_b': _normal(ks[21], (DEPTH, D_MODEL), 0.02),
    }


def reference(x, mem, gdn_w_in, gdn_conv_w, gdn_a_log, gdn_dt_bias, gdn_norm_g,
              mlstm_w_in, mlstm_b_gate, mlstm_norm_g, fox_w_in, fox_b_f, fox_qk_g,
              mem_w_kv, w_out, ln1_g, ln1_b, w_up, w_down, ln2_g, ln2_b):
    for i in range(DEPTH):
        kind, j = i % N_MIXERS, i // N_MIXERS
        if kind == 0:
            mix, memq = gdn_mixer(x, gdn_w_in[j], gdn_conv_w[j], gdn_a_log[j], gdn_dt_bias[j], gdn_norm_g[j])
        elif kind == 1:
            mix, memq = mlstm_mixer(x, mlstm_w_in[j], mlstm_b_gate[j], mlstm_norm_g[j])
        else:
            mix, memq = fox_mixer(x, fox_w_in[j], fox_b_f[j], fox_qk_g[j])
        mem_out = memory_attention(memq, mem, mem_w_kv[i])
        y = jnp.concatenate([mix, mem_out], axis=-1) @ w_out[i]
        x = layer_norm(ALPHA * x + y, ln1_g[i], ln1_b[i])
        hdn = jnp.square(jax.nn.relu(x @ w_up[i]))
        x = layer_norm(ALPHA * x + hdn @ w_down[i], ln2_g[i], ln2_b[i])
    return x
```

```python
import functools
import math

import jax
import jax.numpy as jnp
from jax import lax
from jax.experimental import pallas as pl
from jax.experimental.pallas import tpu as pltpu

F32 = jnp.float32
BF16 = jnp.bfloat16

D_MODEL = 1024
DEPTH = 4
N_MIXERS = 3
GDN_HEADS = 8
GDN_D = 128
GDN_QKV = 3 * GDN_HEADS * GDN_D
CONV_K = 4
MLSTM_HEADS = 8
MLSTM_DQK = 64
MLSTM_DV = 128
FOX_HEADS = 16
FOX_DH = 64
MEM_HEADS = 4
MEM_DH = 128
MEM_W = MEM_HEADS * MEM_DH
D_FF = 4 * D_MODEL
ALPHA = (2 * DEPTH) ** 0.25
LN_EPS = 1e-5
NORM_EPS = 1e-6

LANES = 128
SUBLANES = 8
VMEM_LIMIT = 56 * 1024 * 1024

INV_BLOCK = 16
NEG_INF = float("-inf")


def _cparams(*sem):
    return pltpu.CompilerParams(dimension_semantics=sem, vmem_limit_bytes=VMEM_LIMIT)


def _dot(a, b):
    return jnp.dot(a.astype(BF16), b.astype(BF16), preferred_element_type=F32)


def _dot_nt(a, b):
    return lax.dot_general(a.astype(BF16), b.astype(BF16), (((1,), (1,)), ((), ())),
                           preferred_element_type=F32)


def _dot_tn(a, b):
    return lax.dot_general(a.astype(BF16), b.astype(BF16), (((0,), (0,)), ((), ())),
                           preferred_element_type=F32)


def _sigmoid(x):
    return 1.0 / (1.0 + jnp.exp(-x))


def _softplus(x):
    return jnp.maximum(x, 0.0) + jnp.log1p(jnp.exp(-jnp.abs(x)))


def _iota2(shape, dim):
    return lax.broadcasted_iota(jnp.int32, shape, dim)


def _mm_kernel(x_ref, w_ref, o_ref):
    o_ref[...] = jnp.dot(x_ref[...], w_ref[...], preferred_element_type=F32).astype(o_ref.dtype)


def _matmul(x, w, out_dtype, tm, tn, name):
    m, k = x.shape
    n = w.shape[1]
    tm, tn = min(tm, m), min(tn, n)
    assert m % tm == 0 and n % tn == 0
    return pl.pallas_call(
        _mm_kernel,
        out_shape=jax.ShapeDtypeStruct((m, n), out_dtype),
        grid=(m // tm, n // tn),
        in_specs=[pl.BlockSpec((tm, k), lambda i, j: (i, 0)),
                  pl.BlockSpec((k, tn), lambda i, j: (0, j))],
        out_specs=pl.BlockSpec((tm, tn), lambda i, j: (i, j)),
        compiler_params=_cparams("parallel", "parallel"),
        name=name)(x, w)


def _scan_rows(x, seg):
    pos = _iota2(x.shape, 0) & (seg - 1)
    d = 1
    while d < seg:
        x = x + jnp.where(pos >= d, pltpu.roll(x, d, 0), 0.0)
        d *= 2
    return x


def _scan_lanes(x, seg):
    pos = _iota2(x.shape, 1) & (seg - 1)
    d = 1
    while d < seg:
        x = x + jnp.where(pos >= d, pltpu.roll(x, d, 1), 0.0)
        d *= 2
    return x


def _gate_act(kind, y, p1, p2, idx):
    if kind == "gdn":
        g = -jnp.exp(p1) * _softplus(y + p2)
        val = jnp.where(idx < GDN_HEADS, g, _sigmoid(y))
        scan = idx < GDN_HEADS
    elif kind == "mlstm":
        z = y + p2
        val = jnp.where(idx < MLSTM_HEADS, z, -_softplus(-z))
        scan = idx >= MLSTM_HEADS
    else:
        val = -_softplus(-(y + p2))
        scan = idx >= 0
    return val, scan


def _gates_kernel(x_ref, wg_ref, wgt_ref, p1_ref, p2_ref, p1t_ref, p2t_ref, col_ref, row_ref,
                  ccol_ref, crow_ref, *, kind, seg, n_rows):
    full = kind == "fox"
    tg = x_ref.shape[0]
    x = x_ref[...]
    ycol = jnp.dot(x, wg_ref[...], preferred_element_type=F32)
    yrow = lax.dot_general(wgt_ref[...], x, (((1,), (1,)), ((), ())),
                           preferred_element_type=F32)
    vcol, scol = _gate_act(kind, ycol, p1_ref[...], p2_ref[...], _iota2(ycol.shape, 1))
    vrow, srow = _gate_act(kind, yrow, p1t_ref[...], p2t_ref[...], _iota2(yrow.shape, 0))
    ccol = _scan_rows(vcol, tg if full else seg)
    crow = _scan_lanes(vrow, tg if full else seg)
    if full:
        @pl.when(pl.program_id(1) == 0)
        def _():
            ccol_ref[...] = jnp.zeros_like(ccol_ref)
            crow_ref[...] = jnp.zeros_like(crow_ref)
        ccol = ccol + ccol_ref[...]
        crow = crow + crow_ref[...]
        ccol_ref[...] = ccol[tg - 1:tg, :]
        crow_ref[...] = crow[:, tg - 1:tg]
    col_ref[...] = jnp.where(scol, ccol, vcol)
    row_ref[...] = jnp.where(srow, crow, vrow)[:n_rows, :]


def _gates(xb, wg, p1, p2, kind, seg, bsz, seq, tg):
    n_rows = 2 * SUBLANES
    ng = wg.shape[1]
    wgp = jnp.pad(wg, ((0, 0), (0, LANES - ng))).astype(BF16)
    p1p = jnp.pad(p1, (0, LANES - p1.shape[0])).astype(F32)
    p2p = jnp.pad(p2, (0, LANES - p2.shape[0])).astype(F32)
    tg = min(tg, seq)
    nblk = seq // tg
    t = bsz * seq
    kern = functools.partial(_gates_kernel, kind=kind, seg=seg, n_rows=n_rows)
    const = lambda b, s: (0, 0)
    return pl.pallas_call(
        kern,
        out_shape=(jax.ShapeDtypeStruct((t, LANES), F32), jax.ShapeDtypeStruct((n_rows, t), F32)),
        grid=(bsz, nblk),
        in_specs=[pl.BlockSpec((tg, D_MODEL), lambda b, s: (b * nblk + s, 0)),
                  pl.BlockSpec((D_MODEL, LANES), const),
                  pl.BlockSpec((LANES, D_MODEL), const),
                  pl.BlockSpec((1, LANES), const), pl.BlockSpec((1, LANES), const),
                  pl.BlockSpec((LANES, 1), const), pl.BlockSpec((LANES, 1), const)],
        out_specs=(pl.BlockSpec((tg, LANES), lambda b, s: (b * nblk + s, 0)),
                   pl.BlockSpec((n_rows, tg), lambda b, s: (0, b * nblk + s))),
        scratch_shapes=[pltpu.VMEM((1, LANES), F32), pltpu.VMEM((LANES, 1), F32)],
        compiler_params=_cparams("parallel", "arbitrary"),
        name=f"gates_{kind}")(xb, wgp, wgp.T, p1p[None, :], p2p[None, :], p1p[:, None], p2p[:, None])


def _memattn_kernel(x_ref, wq_ref, kv_ref, o_ref):
    q = jnp.dot(x_ref[...], wq_ref[...], preferred_element_type=F32)
    scale = MEM_DH ** -0.5
    for h in range(MEM_HEADS):
        qh = q[:, h * MEM_DH:(h + 1) * MEM_DH]
        kh = kv_ref[0, :, h * MEM_DH:(h + 1) * MEM_DH]
        vh = kv_ref[0, :, MEM_W + h * MEM_DH:MEM_W + (h + 1) * MEM_DH]
        logits = _dot_nt(qh, kh) * scale
        mx = jnp.max(logits, axis=-1, keepdims=True)
        e = jnp.exp(logits - mx)
        p = e / jnp.sum(e, axis=-1, keepdims=True)
        o_ref[:, h * MEM_DH:(h + 1) * MEM_DH] = _dot(p, vh).astype(o_ref.dtype)


def _mem_attention(xb, wq, kv, bsz, seq, tm):
    tm = min(tm, seq)
    nblk = seq // tm
    n_mem = kv.shape[1]
    return pl.pallas_call(
        _memattn_kernel,
        out_shape=jax.ShapeDtypeStruct((bsz * seq, MEM_W), BF16),
        grid=(bsz, nblk),
        in_specs=[pl.BlockSpec((tm, D_MODEL), lambda b, s: (b * nblk + s, 0)),
                  pl.BlockSpec((D_MODEL, MEM_W), lambda b, s: (0, 0)),
                  pl.BlockSpec((1, n_mem, 2 * MEM_W), lambda b, s: (b, 0, 0))],
        out_specs=pl.BlockSpec((tm, MEM_W), lambda b, s: (b * nblk + s, 0)),
        compiler_params=_cparams("parallel", "parallel"),
        name="mem_attention")(xb, wq, kv)


def _unit_lower_inverse(a):
    c = a.shape[0]
    ri, ci = _iota2((c, c), 0), _iota2((c, c), 1)
    eye = (ri == ci).astype(F32)
    shift = int(math.log2(INV_BLOCK))
    d = jnp.where((ri >> shift) == (ci >> shift), a, 0.0)
    n = a - d
    t = eye - d
    p = d
    for _ in range(shift - 1):
        p = _dot(p, p)
        t = t + _dot(t, p)
    nb = c // INV_BLOCK
    if nb > 1:
        m = _dot(t, n)
        x = eye - m
        p = m
        for _ in range(int(math.log2(nb)) - 1):
            p = _dot(p, p)
            x = x + _dot(x, p)
        t = _dot(x, t)
    return t


def _gdn_kernel(x_ref, gcol_ref, grow_ref, cw_ref, ng_ref, o_ref, ext_ref, qkv_ref, st_ref, *, chunk):
    ts = x_ref.shape[1]
    nchunk = ts // chunk
    s_idx = pl.program_id(1)
    blk = pl.program_id(0) * pl.num_programs(1) + s_idx
    halo = SUBLANES

    @pl.when(s_idx == 0)
    def _():
        st_ref[...] = jnp.zeros_like(st_ref)
        ext_ref[0:halo, :] = jnp.zeros((halo, GDN_QKV), F32)

    ext_ref[halo:halo + ts, :] = x_ref[0, :, 0:GDN_QKV]
    for j in range(GDN_QKV // LANES):
        cols = slice(j * LANES, (j + 1) * LANES)
        acc = cw_ref[CONV_K - 1:CONV_K, cols] * ext_ref[halo:halo + ts, cols]
        for tap in range(CONV_K - 1):
            off = halo - (CONV_K - 1) + tap
            acc = acc + cw_ref[tap:tap + 1, cols] * ext_ref[off:off + ts, cols]
        y = acc * _sigmoid(acc)
        if j < 2 * GDN_HEADS:
            y = y * lax.rsqrt(jnp.sum(y * y, axis=-1, keepdims=True) + NORM_EPS)
        if j < GDN_HEADS:
            y = y * (GDN_D ** -0.5)
        qkv_ref[:, cols] = y
    ext_ref[0:halo, :] = ext_ref[ts:ts + halo, :]

    ri, ci = _iota2((chunk, chunk), 0), _iota2((chunk, chunk), 1)
    incl = ri >= ci
    strict = ri > ci
    ng = ng_ref[...]

    def body(c, carry):
        r0 = pl.multiple_of(c * chunk, chunk)
        rows = pl.ds(r0, chunk)
        gcol = gcol_ref[rows, :]
        cg = blk * nchunk + c
        for h in range(GDN_HEADS):
            hq = slice(h * GDN_D, (h + 1) * GDN_D)
            q = qkv_ref[rows, hq]
            k = qkv_ref[rows, GDN_HEADS * GDN_D + h * GDN_D:GDN_HEADS * GDN_D + (h + 1) * GDN_D]
            v = qkv_ref[rows, 2 * GDN_HEADS * GDN_D + h * GDN_D:2 * GDN_HEADS * GDN_D + (h + 1) * GDN_D]
            gcc = gcol[:, h:h + 1]
            bc = gcol[:, GDN_HEADS + h:GDN_HEADS + h + 1]
            gcr = grow_ref[h, pl.ds(cg, 1), :]
            glast = gcc[chunk - 1:chunk, :]
            egc = jnp.exp(gcc)
            decay = jnp.exp(jnp.where(incl, gcc - gcr, NEG_INF))
            kb = k * bc
            kq = _dot_nt(jnp.concatenate([kb, q], axis=0), k)
            a = jnp.where(strict, kq[:chunk] * decay, 0.0)
            attn = kq[chunk:] * decay
            tinv = _unit_lower_inverse(a)
            sol = _dot(tinv, jnp.concatenate([v * bc, kb * egc], axis=1))
            u, w = sol[:, :GDN_D], sol[:, GDN_D:]
            state = st_ref[h]
            wq = _dot(jnp.concatenate([w, q * egc], axis=0), state)
            v_new = u - wq[:chunk]
            o = wq[chunk:] + _dot(attn, v_new)
            kg = k * jnp.exp(glast - gcc)
            st_ref[h] = state * jnp.exp(glast) + _dot_tn(kg, v_new)
            z = x_ref[0, rows, GDN_QKV + h * GDN_D:GDN_QKV + (h + 1) * GDN_D]
            o = o * lax.rsqrt(jnp.mean(o * o, axis=-1, keepdims=True) + NORM_EPS) * ng
            o_ref[0, rows, hq] = (o * (z * _sigmoid(z))).astype(o_ref.dtype)
        return carry

    lax.fori_loop(0, nchunk, body, 0)


def _gdn_mixer(proj, gcol, grow, conv_w, norm_g, bsz, seq, ts, chunk):
    ts = min(ts, seq)
    nblk = seq // ts
    t = bsz * seq
    ncols = proj.shape[1]
    kern = functools.partial(_gdn_kernel, chunk=chunk)
    grow3 = grow.reshape(grow.shape[0], t // chunk, chunk)
    return pl.pallas_call(
        kern,
        out_shape=jax.ShapeDtypeStruct((bsz, seq, D_MODEL), BF16),
        grid=(bsz, nblk),
        in_specs=[pl.BlockSpec((1, ts, ncols), lambda b, s: (b, s, 0)),
                  pl.BlockSpec((ts, LANES), lambda b, s: (b * nblk + s, 0)),
                  pl.BlockSpec(grow3.shape, lambda b, s: (0, 0, 0)),
                  pl.BlockSpec((CONV_K, GDN_QKV), lambda b, s: (0, 0)),
                  pl.BlockSpec((1, GDN_D), lambda b, s: (0, 0))],
        out_specs=pl.BlockSpec((1, ts, D_MODEL), lambda b, s: (b, s, 0)),
        scratch_shapes=[pltpu.VMEM((ts + 2 * SUBLANES, GDN_QKV), F32),
                        pltpu.VMEM((ts, GDN_QKV), F32),
                        pltpu.VMEM((GDN_HEADS, GDN_D, GDN_D), F32)],
        compiler_params=_cparams("parallel", "arbitrary"),
        name="gdn_mixer")(proj.reshape(bsz, seq, ncols), gcol, grow3, conv_w, norm_g[None, :])


def _mlstm_kernel(x_ref, gcol_ref, grow_ref, ng_ref, o_ref, st_ref, m_ref, *, chunk):
    ts = x_ref.shape[1]
    nchunk = ts // chunk
    s_idx = pl.program_id(1)
    blk = pl.program_id(0) * pl.num_programs(1) + s_idx
    nh, dqk, dv = MLSTM_HEADS, MLSTM_DQK, MLSTM_DV

    @pl.when(s_idx == 0)
    def _():
        st_ref[...] = jnp.zeros_like(st_ref)
        m_ref[...] = jnp.zeros_like(m_ref)

    ri, ci = _iota2((chunk, chunk), 0), _iota2((chunk, chunk), 1)
    incl = ri >= ci
    ones_col = (_iota2((chunk, LANES), 1) == 0).astype(F32)

    def body(c, carry):
        r0 = pl.multiple_of(c * chunk, chunk)
        rows = pl.ds(r0, chunk)
        gcol = gcol_ref[rows, :]
        cg = blk * nchunk + c
        for h in range(nh):
            q = x_ref[0, rows, h * dqk:(h + 1) * dqk] * (dqk ** -0.5)
            k = x_ref[0, rows, nh * dqk + h * dqk:nh * dqk + (h + 1) * dqk]
            v = x_ref[0, rows, 2 * nh * dqk + h * dv:2 * nh * dqk + (h + 1) * dv]
            og = x_ref[0, rows, 2 * nh * dqk + nh * dv + h * dv:2 * nh * dqk + nh * dv + (h + 1) * dv]
            icol = gcol[:, h:h + 1]
            bcol = gcol[:, nh + h:nh + h + 1]
            irow = grow_ref[h, pl.ds(cg, 1), :]
            brow = grow_ref[nh + h, pl.ds(cg, 1), :]
            logd = jnp.where(incl, bcol - brow + irow, NEG_INF)
            m_intra = jnp.max(logd, axis=-1, keepdims=True)
            qk = _dot_nt(q, k)
            blast = bcol[chunk - 1:chunk, :]
            lw = blast - bcol + icol
            mw = jnp.max(lw, axis=0, keepdims=True)
            m_prev = m_ref[h:h + 1, 0:1]
            a_inter = bcol + m_prev
            m_t = jnp.maximum(a_inter, m_intra)
            w_inter = jnp.exp(a_inter - m_t)
            p = qk * jnp.exp(logd - m_t)
            vext = jnp.concatenate([v, ones_col], axis=1)
            state = st_ref[h]
            num = w_inter * _dot(q, state) + _dot(p, vext)
            den = num[:, dv:dv + 1]
            hh = num[:, :dv] / jnp.maximum(jnp.abs(den), jnp.exp(-m_t))
            m_new = jnp.maximum(blast + m_prev, mw)
            keep = jnp.exp(blast + m_prev - m_new)
            kw = k * jnp.exp(lw - m_new)
            st_ref[h] = keep * state + _dot_tn(kw, vext)
            m_ref[h:h + 1, :] = jnp.broadcast_to(m_new, (1, LANES))
            mu = jnp.mean(hh, axis=-1, keepdims=True)
            xc = hh - mu
            var = jnp.mean(xc * xc, axis=-1, keepdims=True)
            hn = xc * lax.rsqrt(var + NORM_EPS) * ng_ref[h:h + 1, :]
            o_ref[0, rows, h * dv:(h + 1) * dv] = (hn * _sigmoid(og)).astype(o_ref.dtype)
        return carry

    lax.fori_loop(0, nchunk, body, 0)


def _mlstm_mixer(proj, gcol, grow, norm_g, bsz, seq, ts, chunk):
    ts = min(ts, seq)
    nblk = seq // ts
    t = bsz * seq
    ncols = proj.shape[1]
    kern = functools.partial(_mlstm_kernel, chunk=chunk)
    grow3 = grow.reshape(grow.shape[0], t // chunk, chunk)
    return pl.pallas_call(
        kern,
        out_shape=jax.ShapeDtypeStruct((bsz, seq, D_MODEL), BF16),
        grid=(bsz, nblk),
        in_specs=[pl.BlockSpec((1, ts, ncols), lambda b, s: (b, s, 0)),
                  pl.BlockSpec((ts, LANES), lambda b, s: (b * nblk + s, 0)),
                  pl.BlockSpec(grow3.shape, lambda b, s: (0, 0, 0)),
                  pl.BlockSpec((MLSTM_HEADS, MLSTM_DV), lambda b, s: (0, 0))],
        out_specs=pl.BlockSpec((1, ts, D_MODEL), lambda b, s: (b, s, 0)),
        scratch_shapes=[pltpu.VMEM((MLSTM_HEADS, MLSTM_DQK, 2 * LANES), F32),
                        pltpu.VMEM((MLSTM_HEADS, LANES), F32)],
        compiler_params=_cparams("parallel", "arbitrary"),
        name="mlstm_mixer")(proj.reshape(bsz, seq, ncols), gcol, grow3, norm_g)


def _fox_prep_kernel(q_ref, k_ref, v_ref, g_ref, qo_ref, ko_ref, vo_ref):
    half = _iota2((q_ref.shape[0], LANES), 1) < FOX_DH

    def norm(x_ref, o_ref, g_row, scale):
        for j in range(x_ref.shape[1] // LANES):
            cols = slice(j * LANES, (j + 1) * LANES)
            x = x_ref[:, cols]
            sq = x * x
            s_lo = jnp.sum(jnp.where(half, sq, 0.0), axis=-1, keepdims=True)
            s_all = jnp.sum(sq, axis=-1, keepdims=True)
            ms = jnp.where(half, s_lo, s_all - s_lo) * (1.0 / FOX_DH)
            y = x * lax.rsqrt(ms + NORM_EPS) * g_ref[g_row:g_row + 1, cols]
            o_ref[:, cols] = (y * scale).astype(o_ref.dtype)

    norm(q_ref, qo_ref, 0, FOX_DH ** -0.5)
    norm(k_ref, ko_ref, 1, 1.0)
    vo_ref[...] = v_ref[...].astype(vo_ref.dtype)


def _fox_prep(proj, qk_g, tp):
    t = proj.shape[0]
    tp = min(tp, t)
    g = jnp.tile(qk_g, (1, FOX_HEADS))
    spec = lambda j: pl.BlockSpec((tp, D_MODEL), lambda i: (i, j))
    out = jax.ShapeDtypeStruct((t, D_MODEL), BF16)
    return pl.pallas_call(
        _fox_prep_kernel,
        out_shape=(out, out, out),
        grid=(t // tp,),
        in_specs=[spec(0), spec(1), spec(2), pl.BlockSpec((2, D_MODEL), lambda i: (0, 0))],
        out_specs=(spec(0), spec(0), spec(0)),
        compiler_params=_cparams("parallel"),
        name="fox_prep")(proj, proj, proj, g)


def _fox_kernel(q_ref, k_ref, v_ref, crow_ref, og_ref, o_ref, m_ref, l_ref, acc_ref, *, tq):
    i = pl.program_id(2)
    hp = pl.program_id(1)
    lane = _iota2((tq, LANES), 1)
    lo = lane < FOX_DH
    q = q_ref[0]
    qs = (jnp.where(lo, q, jnp.zeros_like(q)), jnp.where(lo, jnp.zeros_like(q), q))
    q0 = pl.multiple_of(i * tq, tq)
    m_ref[...] = jnp.full(m_ref.shape, NEG_INF, F32)
    l_ref[...] = jnp.zeros_like(l_ref)
    acc_ref[...] = jnp.zeros_like(acc_ref)
    causal = _iota2((tq, tq), 0) >= _iota2((tq, tq), 1)

    def step(j, masked):
        k0 = pl.multiple_of(j * tq, tq)
        k = k_ref[0, pl.ds(k0, tq), :]
        v = v_ref[0, pl.ds(k0, tq), :]
        for e in range(2):
            hd = 2 * hp + e
            cref = crow_ref[hd, :, pl.ds(q0, LANES)][:, 0:1]
            bias = cref - crow_ref[hd, :, pl.ds(k0, tq)]
            s = lax.dot_general(qs[e], k, (((1,), (1,)), ((), ())), preferred_element_type=F32) + bias
            if masked:
                s = jnp.where(causal, s, NEG_INF)
            m_prev = m_ref[e]
            m_new = jnp.maximum(m_prev, jnp.max(s, axis=-1, keepdims=True))
            alpha = jnp.exp(m_prev - m_new)
            p = jnp.exp(s - m_new)
            l_ref[e] = alpha * l_ref[e] + jnp.sum(p, axis=-1, keepdims=True)
            acc_ref[e] = alpha * acc_ref[e] + jnp.dot(p.astype(BF16), v, preferred_element_type=F32)
            m_ref[e] = m_new

    def body(j, carry):
        step(j, False)
        return carry

    lax.fori_loop(0, i, body, 0)
    step(i, True)
    out = jnp.where(lo, acc_ref[0] / l_ref[0], acc_ref[1] / l_ref[1])
    o_ref[0] = (out * _sigmoid(og_ref[0])).astype(o_ref.dtype)


def _fox_attention(qn, kn, vb, crow, proj, bsz, seq, tq):
    tq = min(tq, seq)
    nq = seq // tq
    npair = FOX_HEADS // 2
    kern = functools.partial(_fox_kernel, tq=tq)
    og_blk = 3 * D_MODEL // LANES
    return pl.pallas_call(
        kern,
        out_shape=jax.ShapeDtypeStruct((bsz, seq, D_MODEL), BF16),
        grid=(bsz, npair, nq),
        in_specs=[pl.BlockSpec((1, tq, LANES), lambda b, h, i: (b, i, h)),
                  pl.BlockSpec((1, seq, LANES), lambda b, h, i: (b, 0, h)),
                  pl.BlockSpec((1, seq, LANES), lambda b, h, i: (b, 0, h)),
                  pl.BlockSpec((FOX_HEADS, 1, seq), lambda b, h, i: (0, 0, b)),
                  pl.BlockSpec((1, tq, LANES), lambda b, h, i: (b, i, og_blk + h))],
        out_specs=pl.BlockSpec((1, tq, LANES), lambda b, h, i: (b, i, h)),
        scratch_shapes=[pltpu.VMEM((2, tq, 1), F32), pltpu.VMEM((2, tq, 1), F32),
                        pltpu.VMEM((2, tq, LANES), F32)],
        compiler_params=_cparams("parallel", "parallel", "arbitrary"),
        name="fox_attention")(qn.reshape(bsz, seq, D_MODEL), kn.reshape(bsz, seq, D_MODEL),
                              vb.reshape(bsz, seq, D_MODEL), crow.reshape(FOX_HEADS, 1, bsz * seq),
                              proj.reshape(bsz, seq, proj.shape[1]))


def _layer_norm(z, g, b):
    mu = jnp.mean(z, axis=-1, keepdims=True)
    zc = z - mu
    var = jnp.mean(zc * zc, axis=-1, keepdims=True)
    return zc * lax.rsqrt(var + LN_EPS) * g + b


def _outproj_kernel(mix_ref, mem_ref, w_ref, x_ref, g_ref, b_ref, o_ref, ob_ref):
    y = jnp.dot(mix_ref[...], w_ref[0:D_MODEL, :], preferred_element_type=F32)
    y = y + jnp.dot(mem_ref[...], w_ref[D_MODEL:D_MODEL + MEM_W, :], preferred_element_type=F32)
    out = _layer_norm(ALPHA * x_ref[...] + y, g_ref[...], b_ref[...])
    o_ref[...] = out
    ob_ref[...] = out.astype(ob_ref.dtype)


def _outproj_ln(mix, memo, w, x, g, b, tm):
    t = x.shape[0]
    tm = min(tm, t)
    row = lambda n: pl.BlockSpec((tm, n), lambda i: (i, 0))
    const = lambda s: pl.BlockSpec(s, lambda i: (0, 0))
    return pl.pallas_call(
        _outproj_kernel,
        out_shape=(jax.ShapeDtypeStruct((t, D_MODEL), F32), jax.ShapeDtypeStruct((t, D_MODEL), BF16)),
        grid=(t // tm,),
        in_specs=[row(D_MODEL), row(MEM_W), const((D_MODEL + MEM_W, D_MODEL)), row(D_MODEL),
                  const((1, D_MODEL)), const((1, D_MODEL))],
        out_specs=(row(D_MODEL), row(D_MODEL)),
        compiler_params=_cparams("parallel"),
        name="outproj_ln")(mix, memo, w, x, g[None, :], b[None, :])


def _ffn_kernel(xb_ref, x_ref, wu_ref, wd_ref, g_ref, b_ref, o_ref, ob_ref, *, fchunk):
    xb = xb_ref[...]
    y = jnp.zeros(x_ref.shape, F32)
    for c in range(D_FF // fchunk):
        h = jnp.dot(xb, wu_ref[:, c * fchunk:(c + 1) * fchunk], preferred_element_type=F32)
        h = jnp.square(jnp.maximum(h, 0.0)).astype(BF16)
        y = y + jnp.dot(h, wd_ref[c * fchunk:(c + 1) * fchunk, :], preferred_element_type=F32)
    out = _layer_norm(ALPHA * x_ref[...] + y, g_ref[...], b_ref[...])
    o_ref[...] = out
    ob_ref[...] = out.astype(ob_ref.dtype)


def _ffn_ln(xb, x, wu, wd, g, b, tm, fchunk):
    t = x.shape[0]
    tm = min(tm, t)
    row = pl.BlockSpec((tm, D_MODEL), lambda i: (i, 0))
    const = lambda s: pl.BlockSpec(s, lambda i: (0, 0), pipeline_mode=pl.Buffered(1))
    kern = functools.partial(_ffn_kernel, fchunk=fchunk)
    return pl.pallas_call(
        kern,
        out_shape=(jax.ShapeDtypeStruct((t, D_MODEL), F32), jax.ShapeDtypeStruct((t, D_MODEL), BF16)),
        grid=(t // tm,),
        in_specs=[row, row, const((D_MODEL, D_FF)), const((D_FF, D_MODEL)),
                  const((1, D_MODEL)), const((1, D_MODEL))],
        out_specs=(row, row),
        compiler_params=_cparams("parallel"),
        name="ffn_ln")(xb, x, wu, wd, g[None, :], b[None, :])


def kernel(x, mem, gdn_w_in, gdn_conv_w, gdn_a_log, gdn_dt_bias, gdn_norm_g, mlstm_w_in, mlstm_b_gate,
           mlstm_norm_g, fox_w_in, fox_b_f, fox_qk_g, mem_w_kv, w_out, ln1_g, ln1_b, w_up, w_down,
           ln2_g, ln2_b):
    bsz, seq, _ = x.shape
    t = bsz * seq
    n_mem = mem.shape[1]
    x = x.reshape(t, D_MODEL)
    xb = x.astype(BF16)
    memb = mem.reshape(bsz * n_mem, D_MODEL).astype(BF16)
    chunk = 64
    for i in range(DEPTH):
        kind, j = i % N_MIXERS, i // N_MIXERS
        if kind == 0:
            w_in, n_main, n_gate = gdn_w_in[j], 4 * D_MODEL, 2 * GDN_HEADS
        elif kind == 1:
            w_in, n_main, n_gate = mlstm_w_in[j], 3 * D_MODEL, 2 * MLSTM_HEADS
        else:
            w_in, n_main, n_gate = fox_w_in[j], 4 * D_MODEL, FOX_HEADS
        w_main = w_in[:, :n_main].astype(BF16)
        w_gate = w_in[:, n_main:n_main + n_gate]
        w_memq = w_in[:, n_main + n_gate:].astype(BF16)
        proj = _matmul(xb, w_main, F32, 1024, 1024, "in_proj")
        kv = _matmul(memb, mem_w_kv[i].astype(BF16), BF16, 512, 1024, "mem_kv")
        memo = _mem_attention(xb, w_memq, kv.reshape(bsz, n_mem, 2 * MEM_W), bsz, seq, 512)
        if kind == 0:
            gcol, grow = _gates(xb, w_gate, gdn_a_log[j], gdn_dt_bias[j], "gdn", chunk, bsz, seq, 512)
            mix = _gdn_mixer(proj, gcol, grow, gdn_conv_w[j], gdn_norm_g[j], bsz, seq, 512, chunk)
        elif kind == 1:
            bias = jnp.concatenate([mlstm_b_gate[j, 0], mlstm_b_gate[j, 1]])
            gcol, grow = _gates(xb, w_gate, bias, bias, "mlstm", chunk, bsz, seq, 512)
            mix = _mlstm_mixer(proj, gcol, grow, mlstm_norm_g[j], bsz, seq, 512, chunk)
        else:
            _, crow = _gates(xb, w_gate, fox_b_f[j], fox_b_f[j], "fox", chunk, bsz, seq, 512)
            qn, kn, vb = _fox_prep(proj, fox_qk_g[j], 512)
            mix = _fox_attention(qn, kn, vb, crow, proj, bsz, seq, 512)
        x, xb = _outproj_ln(mix.reshape(t, D_MODEL), memo, w_out[i].astype(BF16), x, ln1_g[i], ln1_b[i], 512)
        x, xb = _ffn_ln(xb, x, w_up[i].astype(BF16), w_down[i].astype(BF16), ln2_g[i], ln2_b[i], 512, 1024)
    return x.reshape(bsz, seq, D_MODEL)
```

```python
import functools
import math

import jax
import jax.numpy as jnp
from jax import lax
from jax.experimental import pallas as pl
from jax.experimental.pallas import tpu as pltpu

F32 = jnp.float32
BF16 = jnp.bfloat16

D_MODEL = 1024
DEPTH = 4
N_MIXERS = 3
GDN_HEADS = 8
GDN_D = 128
GDN_QKV = 3 * GDN_HEADS * GDN_D
CONV_K = 4
MLSTM_HEADS = 8
MLSTM_DQK = 64
MLSTM_DV = 128
FOX_HEADS = 16
FOX_DH = 64
MEM_HEADS = 4
MEM_DH = 128
MEM_W = MEM_HEADS * MEM_DH
D_FF = 4 * D_MODEL
ALPHA = (2 * DEPTH) ** 0.25
LN_EPS = 1e-5
NORM_EPS = 1e-6

LANES = 128
SUBLANES = 8
VMEM_LIMIT = 56 * 1024 * 1024

INV_BLOCK = 16
NEG_INF = float("-inf")
LOG2E = math.log2(math.e)


def _cparams(*sem):
    return pltpu.CompilerParams(dimension_semantics=sem, vmem_limit_bytes=VMEM_LIMIT)


def _dot(a, b):
    return jnp.dot(a.astype(BF16), b.astype(BF16), preferred_element_type=F32)


def _dot_nt(a, b):
    return lax.dot_general(a.astype(BF16), b.astype(BF16), (((1,), (1,)), ((), ())),
                           preferred_element_type=F32)


def _dot_tn(a, b):
    return lax.dot_general(a.astype(BF16), b.astype(BF16), (((0,), (0,)), ((), ())),
                           preferred_element_type=F32)


def _sigmoid(x):
    return 1.0 / (1.0 + jnp.exp(-x))


def _softplus(x):
    return jnp.maximum(x, 0.0) + jnp.log1p(jnp.exp(-jnp.abs(x)))


def _iota2(shape, dim):
    return lax.broadcasted_iota(jnp.int32, shape, dim)


def _mm_kernel(x_ref, w_ref, o_ref):
    o_ref[...] = jnp.dot(x_ref[...], w_ref[...], preferred_element_type=F32).astype(o_ref.dtype)


def _matmul(x, w, out_dtype, tm, tn, name):
    m, k = x.shape
    n = w.shape[1]
    tm, tn = min(tm, m), min(tn, n)
    assert m % tm == 0 and n % tn == 0
    return pl.pallas_call(
        _mm_kernel,
        out_shape=jax.ShapeDtypeStruct((m, n), out_dtype),
        grid=(m // tm, n // tn),
        in_specs=[pl.BlockSpec((tm, k), lambda i, j: (i, 0)),
                  pl.BlockSpec((k, tn), lambda i, j: (0, j))],
        out_specs=pl.BlockSpec((tm, tn), lambda i, j: (i, j)),
        compiler_params=_cparams("parallel", "parallel"),
        name=name)(x, w)


def _scan_rows(x, seg):
    pos = _iota2(x.shape, 0) & (seg - 1)
    d = 1
    while d < seg:
        x = x + jnp.where(pos >= d, pltpu.roll(x, d, 0), 0.0)
        d *= 2
    return x


def _scan_lanes(x, seg):
    pos = _iota2(x.shape, 1) & (seg - 1)
    d = 1
    while d < seg:
        x = x + jnp.where(pos >= d, pltpu.roll(x, d, 1), 0.0)
        d *= 2
    return x


def _gate_act(kind, y, p1, p2, idx):
    if kind == "gdn":
        g = -jnp.exp(p1) * _softplus(y + p2)
        val = jnp.where(idx < GDN_HEADS, g, _sigmoid(y))
        scan = idx < GDN_HEADS
    elif kind == "mlstm":
        z = y + p2
        val = jnp.where(idx < MLSTM_HEADS, z, -_softplus(-z))
        scan = idx >= MLSTM_HEADS
    else:
        val = -_softplus(-(y + p2))
        scan = idx >= 0
    return val, scan


def _gates_kernel(x_ref, wg_ref, wgt_ref, p1_ref, p2_ref, p1t_ref, p2t_ref, col_ref, row_ref,
                  ccol_ref, crow_ref, *, kind, seg, n_rows):
    full = kind == "fox"
    tg = x_ref.shape[0]
    x = x_ref[...]
    ycol = jnp.dot(x, wg_ref[...], preferred_element_type=F32)
    yrow = lax.dot_general(wgt_ref[...], x, (((1,), (1,)), ((), ())),
                           preferred_element_type=F32)
    vcol, scol = _gate_act(kind, ycol, p1_ref[...], p2_ref[...], _iota2(ycol.shape, 1))
    vrow, srow = _gate_act(kind, yrow, p1t_ref[...], p2t_ref[...], _iota2(yrow.shape, 0))
    ccol = _scan_rows(vcol, tg if full else seg)
    crow = _scan_lanes(vrow, tg if full else seg)
    if full:
        @pl.when(pl.program_id(1) == 0)
        def _():
            ccol_ref[...] = jnp.zeros_like(ccol_ref)
            crow_ref[...] = jnp.zeros_like(crow_ref)
        ccol = ccol + ccol_ref[...]
        crow = crow + crow_ref[...]
        ccol_ref[...] = ccol[tg - 1:tg, :]
        crow_ref[...] = crow[:, tg - 1:tg]
    col_ref[...] = jnp.where(scol, ccol, vcol)
    row_ref[...] = jnp.where(srow, crow, vrow)[:n_rows, :]


def _gates(xb, wg, p1, p2, kind, seg, bsz, seq, tg):
    n_rows = 2 * SUBLANES
    ng = wg.shape[1]
    wgp = jnp.pad(wg, ((0, 0), (0, LANES - ng))).astype(BF16)
    p1p = jnp.pad(p1, (0, LANES - p1.shape[0])).astype(F32)
    p2p = jnp.pad(p2, (0, LANES - p2.shape[0])).astype(F32)
    tg = min(tg, seq)
    nblk = seq // tg
    t = bsz * seq
    kern = functools.partial(_gates_kernel, kind=kind, seg=seg, n_rows=n_rows)
    const = lambda b, s: (0, 0)
    return pl.pallas_call(
        kern,
        out_shape=(jax.ShapeDtypeStruct((t, LANES), F32), jax.ShapeDtypeStruct((n_rows, t), F32)),
        grid=(bsz, nblk),
        in_specs=[pl.BlockSpec((tg, D_MODEL), lambda b, s: (b * nblk + s, 0)),
                  pl.BlockSpec((D_MODEL, LANES), const),
                  pl.BlockSpec((LANES, D_MODEL), const),
                  pl.BlockSpec((1, LANES), const), pl.BlockSpec((1, LANES), const),
                  pl.BlockSpec((LANES, 1), const), pl.BlockSpec((LANES, 1), const)],
        out_specs=(pl.BlockSpec((tg, LANES), lambda b, s: (b * nblk + s, 0)),
                   pl.BlockSpec((n_rows, tg), lambda b, s: (0, b * nblk + s))),
        scratch_shapes=[pltpu.VMEM((1, LANES), F32), pltpu.VMEM((LANES, 1), F32)],
        compiler_params=_cparams("parallel", "arbitrary"),
        name=f"gates_{kind}")(xb, wgp, wgp.T, p1p[None, :], p2p[None, :], p1p[:, None], p2p[:, None])


def _memattn_kernel(x_ref, wq_ref, kv_ref, o_ref):
    q = jnp.dot(x_ref[...], wq_ref[...], preferred_element_type=F32)
    scale = MEM_DH ** -0.5
    for h in range(MEM_HEADS):
        qh = q[:, h * MEM_DH:(h + 1) * MEM_DH]
        kh = kv_ref[0, :, h * MEM_DH:(h + 1) * MEM_DH]
        vh = kv_ref[0, :, MEM_W + h * MEM_DH:MEM_W + (h + 1) * MEM_DH]
        logits = _dot_nt(qh, kh) * scale
        mx = jnp.max(logits, axis=-1, keepdims=True)
        e = jnp.exp(logits - mx)
        p = e / jnp.sum(e, axis=-1, keepdims=True)
        o_ref[:, h * MEM_DH:(h + 1) * MEM_DH] = _dot(p, vh).astype(o_ref.dtype)


def _mem_attention(xb, wq, kv, bsz, seq, tm):
    tm = min(tm, seq)
    nblk = seq // tm
    n_mem = kv.shape[1]
    return pl.pallas_call(
        _memattn_kernel,
        out_shape=jax.ShapeDtypeStruct((bsz * seq, MEM_W), BF16),
        grid=(bsz, nblk),
        in_specs=[pl.BlockSpec((tm, D_MODEL), lambda b, s: (b * nblk + s, 0)),
                  pl.BlockSpec((D_MODEL, MEM_W), lambda b, s: (0, 0)),
                  pl.BlockSpec((1, n_mem, 2 * MEM_W), lambda b, s: (b, 0, 0))],
        out_specs=pl.BlockSpec((tm, MEM_W), lambda b, s: (b * nblk + s, 0)),
        compiler_params=_cparams("parallel", "parallel"),
        name="mem_attention")(xb, wq, kv)


def _unit_lower_inverse(a_list):
    c = a_list[0].shape[0]
    ri, ci = _iota2((c, c), 0), _iota2((c, c), 1)
    eye = (ri == ci).astype(F32)
    shift = int(math.log2(INV_BLOCK))
    same_block = (ri >> shift) == (ci >> shift)
    d = [jnp.where(same_block, a, 0.0) for a in a_list]
    n = [a - di for a, di in zip(a_list, d)]
    t = [eye - di for di in d]
    p = d
    for _ in range(shift - 1):
        p = [_dot(pi, pi) for pi in p]
        t = [ti + _dot(ti, pi) for ti, pi in zip(t, p)]
    nb = c // INV_BLOCK
    if nb > 1:
        m = [_dot(ti, ni) for ti, ni in zip(t, n)]
        x = [eye - mi for mi in m]
        p = m
        for _ in range(int(math.log2(nb)) - 1):
            p = [_dot(pi, pi) for pi in p]
            x = [xi + _dot(xi, pi) for xi, pi in zip(x, p)]
        t = [_dot(xi, ti) for xi, ti in zip(x, t)]
    return t


def _gdn_kernel(x_ref, gcol_ref, grow_ref, cw_ref, ng_ref, o_ref, ext_ref, qkv_ref, st_ref, *, chunk):
    ts = x_ref.shape[1]
    nchunk = ts // chunk
    s_idx = pl.program_id(1)
    blk = pl.program_id(0) * pl.num_programs(1) + s_idx
    halo = SUBLANES

    @pl.when(s_idx == 0)
    def _():
        st_ref[...] = jnp.zeros_like(st_ref)
        ext_ref[0:halo, :] = jnp.zeros((halo, GDN_QKV), F32)

    ext_ref[halo:halo + ts, :] = x_ref[0, :, 0:GDN_QKV]
    for j in range(GDN_QKV // LANES):
        cols = slice(j * LANES, (j + 1) * LANES)
        acc = cw_ref[CONV_K - 1:CONV_K, cols] * ext_ref[halo:halo + ts, cols]
        for tap in range(CONV_K - 1):
            off = halo - (CONV_K - 1) + tap
            acc = acc + cw_ref[tap:tap + 1, cols] * ext_ref[off:off + ts, cols]
        y = acc * _sigmoid(acc)
        if j < 2 * GDN_HEADS:
            y = y * lax.rsqrt(jnp.sum(y * y, axis=-1, keepdims=True) + NORM_EPS)
        if j < GDN_HEADS:
            y = y * (GDN_D ** -0.5)
        qkv_ref[:, cols] = y
    ext_ref[0:halo, :] = ext_ref[ts:ts + halo, :]

    ri, ci = _iota2((chunk, chunk), 0), _iota2((chunk, chunk), 1)
    incl = ri >= ci
    strict = ri > ci
    ng = ng_ref[...]

    def body(c, carry):
        r0 = pl.multiple_of(c * chunk, chunk)
        rows = pl.ds(r0, chunk)
        gcol = gcol_ref[rows, :]
        cg = blk * nchunk + c
        heads = range(GDN_HEADS)
        nqk = GDN_HEADS * GDN_D
        q = [qkv_ref[rows, h * GDN_D:(h + 1) * GDN_D] for h in heads]
        k = [qkv_ref[rows, nqk + h * GDN_D:nqk + (h + 1) * GDN_D] for h in heads]
        v = [qkv_ref[rows, 2 * nqk + h * GDN_D:2 * nqk + (h + 1) * GDN_D] for h in heads]
        gcc = [gcol[:, h:h + 1] for h in heads]
        bc = [gcol[:, GDN_HEADS + h:GDN_HEADS + h + 1] for h in heads]
        glast = [g[chunk - 1:chunk, :] for g in gcc]
        egc = [jnp.exp(g) for g in gcc]
        decay = [jnp.exp(jnp.where(incl, gcc[h] - grow_ref[h, pl.ds(cg, 1), :], NEG_INF)) for h in heads]
        kb = [k[h] * bc[h] for h in heads]
        kq = [_dot_nt(jnp.concatenate([kb[h], q[h]], axis=0), k[h]) for h in heads]
        a = [jnp.where(strict, kq[h][:chunk] * decay[h], 0.0) for h in heads]
        attn = [kq[h][chunk:] * decay[h] for h in heads]
        tinv = _unit_lower_inverse(a)
        sol = [_dot(tinv[h], jnp.concatenate([v[h] * bc[h], kb[h] * egc[h]], axis=1)) for h in heads]
        state = [st_ref[h] for h in heads]
        wq = [_dot(jnp.concatenate([sol[h][:, GDN_D:], q[h] * egc[h]], axis=0), state[h]) for h in heads]
        v_new = [sol[h][:, :GDN_D] - wq[h][:chunk] for h in heads]
        o = [wq[h][chunk:] + _dot(attn[h], v_new[h]) for h in heads]
        for h in heads:
            kg = k[h] * jnp.exp(glast[h] - gcc[h])
            st_ref[h] = state[h] * jnp.exp(glast[h]) + _dot_tn(kg, v_new[h])
        for h in heads:
            z = x_ref[0, rows, GDN_QKV + h * GDN_D:GDN_QKV + (h + 1) * GDN_D]
            on = o[h] * lax.rsqrt(jnp.mean(o[h] * o[h], axis=-1, keepdims=True) + NORM_EPS) * ng
            o_ref[0, rows, h * GDN_D:(h + 1) * GDN_D] = (on * (z * _sigmoid(z))).astype(o_ref.dtype)
        return carry

    lax.fori_loop(0, nchunk, body, 0)


def _gdn_mixer(proj, gcol, grow, conv_w, norm_g, bsz, seq, ts, chunk):
    ts = min(ts, seq)
    nblk = seq // ts
    t = bsz * seq
    ncols = proj.shape[1]
    kern = functools.partial(_gdn_kernel, chunk=chunk)
    grow3 = grow.reshape(grow.shape[0], t // chunk, chunk)
    return pl.pallas_call(
        kern,
        out_shape=jax.ShapeDtypeStruct((bsz, seq, D_MODEL), BF16),
        grid=(bsz, nblk),
        in_specs=[pl.BlockSpec((1, ts, ncols), lambda b, s: (b, s, 0)),
                  pl.BlockSpec((ts, LANES), lambda b, s: (b * nblk + s, 0)),
                  pl.BlockSpec(grow3.shape, lambda b, s: (0, 0, 0)),
                  pl.BlockSpec((CONV_K, GDN_QKV), lambda b, s: (0, 0)),
                  pl.BlockSpec((1, GDN_D), lambda b, s: (0, 0))],
        out_specs=pl.BlockSpec((1, ts, D_MODEL), lambda b, s: (b, s, 0)),
        scratch_shapes=[pltpu.VMEM((ts + 2 * SUBLANES, GDN_QKV), F32),
                        pltpu.VMEM((ts, GDN_QKV), F32),
                        pltpu.VMEM((GDN_HEADS, GDN_D, GDN_D), F32)],
        compiler_params=_cparams("parallel", "arbitrary"),
        name="gdn_mixer")(proj.reshape(bsz, seq, ncols), gcol, grow3, conv_w, norm_g[None, :])


def _mlstm_kernel(x_ref, gcol_ref, grow_ref, ng_ref, o_ref, st_ref, m_ref, *, chunk):
    ts = x_ref.shape[1]
    nchunk = ts // chunk
    s_idx = pl.program_id(1)
    blk = pl.program_id(0) * pl.num_programs(1) + s_idx
    nh, dqk, dv = MLSTM_HEADS, MLSTM_DQK, MLSTM_DV

    @pl.when(s_idx == 0)
    def _():
        st_ref[...] = jnp.zeros_like(st_ref)
        m_ref[...] = jnp.zeros_like(m_ref)

    ri, ci = _iota2((chunk, chunk), 0), _iota2((chunk, chunk), 1)
    incl = ri >= ci
    ones_col = (_iota2((chunk, LANES), 1) == 0).astype(F32)

    def body(c, carry):
        r0 = pl.multiple_of(c * chunk, chunk)
        rows = pl.ds(r0, chunk)
        gcol = gcol_ref[rows, :]
        cg = blk * nchunk + c
        heads = range(nh)
        v0, og0 = 2 * nh * dqk, 2 * nh * dqk + nh * dv
        q = [x_ref[0, rows, h * dqk:(h + 1) * dqk] * (dqk ** -0.5) for h in heads]
        k = [x_ref[0, rows, nh * dqk + h * dqk:nh * dqk + (h + 1) * dqk] for h in heads]
        vext = [jnp.concatenate([x_ref[0, rows, v0 + h * dv:v0 + (h + 1) * dv], ones_col], axis=1)
                for h in heads]
        icol = [gcol[:, h:h + 1] for h in heads]
        bcol = [gcol[:, nh + h:nh + h + 1] for h in heads]
        logd = [jnp.where(incl, bcol[h] - grow_ref[nh + h, pl.ds(cg, 1), :] + grow_ref[h, pl.ds(cg, 1), :],
                          NEG_INF) for h in heads]
        m_intra = [jnp.max(ld, axis=-1, keepdims=True) for ld in logd]
        qk = [_dot_nt(q[h], k[h]) for h in heads]
        state = [st_ref[h] for h in heads]
        qs = [_dot(q[h], state[h]) for h in heads]
        blast = [b[chunk - 1:chunk, :] for b in bcol]
        lw = [blast[h] - bcol[h] + icol[h] for h in heads]
        mw = [jnp.max(x, axis=0, keepdims=True) for x in lw]
        m_prev = [m_ref[h:h + 1, 0:1] for h in heads]
        a_inter = [bcol[h] + m_prev[h] for h in heads]
        m_t = [jnp.maximum(a_inter[h], m_intra[h]) for h in heads]
        p = [qk[h] * jnp.exp(logd[h] - m_t[h]) for h in heads]
        pv = [_dot(p[h], vext[h]) for h in heads]
        m_new = [jnp.maximum(blast[h] + m_prev[h], mw[h]) for h in heads]
        for h in heads:
            kw = k[h] * jnp.exp(lw[h] - m_new[h])
            st_ref[h] = jnp.exp(blast[h] + m_prev[h] - m_new[h]) * state[h] + _dot_tn(kw, vext[h])
            m_ref[h:h + 1, :] = jnp.broadcast_to(m_new[h], (1, LANES))
        for h in heads:
            num = jnp.exp(a_inter[h] - m_t[h]) * qs[h] + pv[h]
            den = num[:, dv:dv + 1]
            hh = num[:, :dv] / jnp.maximum(jnp.abs(den), jnp.exp(-m_t[h]))
            mu = jnp.mean(hh, axis=-1, keepdims=True)
            xc = hh - mu
            var = jnp.mean(xc * xc, axis=-1, keepdims=True)
            hn = xc * lax.rsqrt(var + NORM_EPS) * ng_ref[h:h + 1, :]
            og = x_ref[0, rows, og0 + h * dv:og0 + (h + 1) * dv]
            o_ref[0, rows, h * dv:(h + 1) * dv] = (hn * _sigmoid(og)).astype(o_ref.dtype)
        return carry

    lax.fori_loop(0, nchunk, body, 0)


def _mlstm_mixer(proj, gcol, grow, norm_g, bsz, seq, ts, chunk):
    ts = min(ts, seq)
    nblk = seq // ts
    t = bsz * seq
    ncols = proj.shape[1]
    kern = functools.partial(_mlstm_kernel, chunk=chunk)
    grow3 = grow.reshape(grow.shape[0], t // chunk, chunk)
    return pl.pallas_call(
        kern,
        out_shape=jax.ShapeDtypeStruct((bsz, seq, D_MODEL), BF16),
        grid=(bsz, nblk),
        in_specs=[pl.BlockSpec((1, ts, ncols), lambda b, s: (b, s, 0)),
                  pl.BlockSpec((ts, LANES), lambda b, s: (b * nblk + s, 0)),
                  pl.BlockSpec(grow3.shape, lambda b, s: (0, 0, 0)),
                  pl.BlockSpec((MLSTM_HEADS, MLSTM_DV), lambda b, s: (0, 0))],
        out_specs=pl.BlockSpec((1, ts, D_MODEL), lambda b, s: (b, s, 0)),
        scratch_shapes=[pltpu.VMEM((MLSTM_HEADS, MLSTM_DQK, 2 * LANES), F32),
                        pltpu.VMEM((MLSTM_HEADS, LANES), F32)],
        compiler_params=_cparams("parallel", "arbitrary"),
        name="mlstm_mixer")(proj.reshape(bsz, seq, ncols), gcol, grow3, norm_g)


def _fox_prep_kernel(q_ref, k_ref, v_ref, g_ref, qo_ref, ko_ref, vo_ref):
    half = _iota2((q_ref.shape[0], LANES), 1) < FOX_DH

    def norm(x_ref, o_ref, g_row, scale):
        for j in range(x_ref.shape[1] // LANES):
            cols = slice(j * LANES, (j + 1) * LANES)
            x = x_ref[:, cols]
            sq = x * x
            s_lo = jnp.sum(jnp.where(half, sq, 0.0), axis=-1, keepdims=True)
            s_all = jnp.sum(sq, axis=-1, keepdims=True)
            ms = jnp.where(half, s_lo, s_all - s_lo) * (1.0 / FOX_DH)
            y = x * lax.rsqrt(ms + NORM_EPS) * g_ref[g_row:g_row + 1, cols]
            o_ref[:, cols] = (y * scale).astype(o_ref.dtype)

    norm(q_ref, qo_ref, 0, FOX_DH ** -0.5 * LOG2E)
    norm(k_ref, ko_ref, 1, 1.0)
    vo_ref[...] = v_ref[...].astype(vo_ref.dtype)


def _fox_prep(proj, qk_g, tp):
    t = proj.shape[0]
    tp = min(tp, t)
    g = jnp.tile(qk_g, (1, FOX_HEADS))
    spec = lambda j: pl.BlockSpec((tp, D_MODEL), lambda i: (i, j))
    out = jax.ShapeDtypeStruct((t, D_MODEL), BF16)
    return pl.pallas_call(
        _fox_prep_kernel,
        out_shape=(out, out, out),
        grid=(t // tp,),
        in_specs=[spec(0), spec(1), spec(2), pl.BlockSpec((2, D_MODEL), lambda i: (0, 0))],
        out_specs=(spec(0), spec(0), spec(0)),
        compiler_params=_cparams("parallel"),
        name="fox_prep")(proj, proj, proj, g)


def _fox_kernel(q_ref, k_ref, v_ref, crow_ref, og_ref, o_ref, m_ref, l_ref, acc_ref, *, tq, wide):
    i = pl.program_id(2)
    hp = pl.program_id(1)
    pair = range(2)
    lo = _iota2((tq, LANES), 1) < FOX_DH
    q = q_ref[0]
    zero = jnp.zeros_like(q)
    qs = (jnp.where(lo, q, zero), jnp.where(lo, zero, q))
    q0 = pl.multiple_of(i * tq, tq)
    cref = [crow_ref[2 * hp + e, :, pl.ds(q0, LANES)][:, 0:1] for e in pair]
    m_ref[...] = jnp.full(m_ref.shape, NEG_INF, F32)
    l_ref[...] = jnp.zeros_like(l_ref)
    acc_ref[...] = jnp.zeros_like(acc_ref)
    causal = _iota2((tq, tq), 0) >= _iota2((tq, tq), 1)

    def step(k0, width, masked):
        k = k_ref[0, pl.ds(k0, width), :]
        v = v_ref[0, pl.ds(k0, width), :]
        s = [lax.dot_general(qs[e], k, (((1,), (1,)), ((), ())), preferred_element_type=F32) for e in pair]
        for e in pair:
            bias = (cref[e] - crow_ref[2 * hp + e, :, pl.ds(k0, width)]) * LOG2E
            s[e] = s[e] + bias
            if masked:
                s[e] = jnp.where(causal, s[e], NEG_INF)
        m_prev = [m_ref[e] for e in pair]
        m_new = [jnp.maximum(m_prev[e], jnp.max(s[e], axis=-1, keepdims=True)) for e in pair]
        p = [jnp.exp2(s[e] - jnp.concatenate([m_new[e]] * (width // LANES), axis=1)) for e in pair]
        pv = [jnp.dot(p[e].astype(BF16), v, preferred_element_type=F32) for e in pair]
        for e in pair:
            alpha = jnp.exp2(m_prev[e] - m_new[e])
            l_ref[e] = alpha * l_ref[e] + jnp.sum(p[e], axis=-1, keepdims=True)
            acc_ref[e] = alpha * acc_ref[e] + pv[e]
            m_ref[e] = m_new[e]

    def body(j, carry):
        step(pl.multiple_of(j * (wide * tq), wide * tq), wide * tq, False)
        return carry

    lax.fori_loop(0, i // wide, body, 0)
    if wide == 2:
        @pl.when(i % 2 == 1)
        def _():
            step(pl.multiple_of((i - 1) * tq, tq), tq, False)
    step(q0, tq, True)
    out = jnp.where(lo, acc_ref[0] / l_ref[0], acc_ref[1] / l_ref[1])
    o_ref[0] = (out * _sigmoid(og_ref[0])).astype(o_ref.dtype)


def _fox_attention(qn, kn, vb, crow, proj, bsz, seq, tq, wide):
    tq = min(tq, seq)
    nq = seq // tq
    npair = FOX_HEADS // 2
    kern = functools.partial(_fox_kernel, tq=tq, wide=wide)
    og_blk = 3 * D_MODEL // LANES
    return pl.pallas_call(
        kern,
        out_shape=jax.ShapeDtypeStruct((bsz, seq, D_MODEL), BF16),
        grid=(bsz, npair, nq),
        in_specs=[pl.BlockSpec((1, tq, LANES), lambda b, h, i: (b, i, h)),
                  pl.BlockSpec((1, seq, LANES), lambda b, h, i: (b, 0, h)),
                  pl.BlockSpec((1, seq, LANES), lambda b, h, i: (b, 0, h)),
                  pl.BlockSpec((FOX_HEADS, 1, seq), lambda b, h, i: (0, 0, b)),
                  pl.BlockSpec((1, tq, LANES), lambda b, h, i: (b, i, og_blk + h))],
        out_specs=pl.BlockSpec((1, tq, LANES), lambda b, h, i: (b, i, h)),
        scratch_shapes=[pltpu.VMEM((2, tq, LANES), F32), pltpu.VMEM((2, tq, LANES), F32),
                        pltpu.VMEM((2, tq, LANES), F32)],
        compiler_params=_cparams("parallel", "parallel", "arbitrary"),
        name="fox_attention")(qn.reshape(bsz, seq, D_MODEL), kn.reshape(bsz, seq, D_MODEL),
                              vb.reshape(bsz, seq, D_MODEL), crow.reshape(FOX_HEADS, 1, bsz * seq),
                              proj.reshape(bsz, seq, proj.shape[1]))


def _layer_norm(z, g, b):
    mu = jnp.mean(z, axis=-1, keepdims=True)
    zc = z - mu
    var = jnp.mean(zc * zc, axis=-1, keepdims=True)
    return zc * lax.rsqrt(var + LN_EPS) * g + b


def _outproj_kernel(mix_ref, mem_ref, w_ref, x_ref, g_ref, b_ref, o_ref, ob_ref):
    y = jnp.dot(mix_ref[...], w_ref[0:D_MODEL, :], preferred_element_type=F32)
    y = y + jnp.dot(mem_ref[...], w_ref[D_MODEL:D_MODEL + MEM_W, :], preferred_element_type=F32)
    out = _layer_norm(ALPHA * x_ref[...] + y, g_ref[...], b_ref[...])
    o_ref[...] = out
    ob_ref[...] = out.astype(ob_ref.dtype)


def _outproj_ln(mix, memo, w, x, g, b, tm):
    t = x.shape[0]
    tm = min(tm, t)
    row = lambda n: pl.BlockSpec((tm, n), lambda i: (i, 0))
    const = lambda s: pl.BlockSpec(s, lambda i: (0, 0))
    return pl.pallas_call(
        _outproj_kernel,
        out_shape=(jax.ShapeDtypeStruct((t, D_MODEL), F32), jax.ShapeDtypeStruct((t, D_MODEL), BF16)),
        grid=(t // tm,),
        in_specs=[row(D_MODEL), row(MEM_W), const((D_MODEL + MEM_W, D_MODEL)), row(D_MODEL),
                  const((1, D_MODEL)), const((1, D_MODEL))],
        out_specs=(row(D_MODEL), row(D_MODEL)),
        compiler_params=_cparams("parallel"),
        name="outproj_ln")(mix, memo, w, x, g[None, :], b[None, :])


def _ffn_kernel(xb_ref, x_ref, wu_ref, wd_ref, g_ref, b_ref, o_ref, ob_ref, *, fchunk):
    xb = xb_ref[...]
    y = jnp.zeros(x_ref.shape, F32)
    for c in range(D_FF // fchunk):
        h = jnp.dot(xb, wu_ref[:, c * fchunk:(c + 1) * fchunk], preferred_element_type=F32)
        h = jnp.square(jnp.maximum(h, 0.0)).astype(BF16)
        y = y + jnp.dot(h, wd_ref[c * fchunk:(c + 1) * fchunk, :], preferred_element_type=F32)
    out = _layer_norm(ALPHA * x_ref[...] + y, g_ref[...], b_ref[...])
    o_ref[...] = out
    ob_ref[...] = out.astype(ob_ref.dtype)


def _ffn_ln(xb, x, wu, wd, g, b, tm, fchunk):
    t = x.shape[0]
    tm = min(tm, t)
    row = pl.BlockSpec((tm, D_MODEL), lambda i: (i, 0))
    const = lambda s: pl.BlockSpec(s, lambda i: (0, 0), pipeline_mode=pl.Buffered(1))
    kern = functools.partial(_ffn_kernel, fchunk=fchunk)
    return pl.pallas_call(
        kern,
        out_shape=(jax.ShapeDtypeStruct((t, D_MODEL), F32), jax.ShapeDtypeStruct((t, D_MODEL), BF16)),
        grid=(t // tm,),
        in_specs=[row, row, const((D_MODEL, D_FF)), const((D_FF, D_MODEL)),
                  const((1, D_MODEL)), const((1, D_MODEL))],
        out_specs=(row, row),
        compiler_params=_cparams("parallel"),
        name="ffn_ln")(xb, x, wu, wd, g[None, :], b[None, :])


def kernel(x, mem, gdn_w_in, gdn_conv_w, gdn_a_log, gdn_dt_bias, gdn_norm_g, mlstm_w_in, mlstm_b_gate,
           mlstm_norm_g, fox_w_in, fox_b_f, fox_qk_g, mem_w_kv, w_out, ln1_g, ln1_b, w_up, w_down,
           ln2_g, ln2_b):
    bsz, seq, _ = x.shape
    t = bsz * seq
    n_mem = mem.shape[1]
    x = x.reshape(t, D_MODEL)
    xb = x.astype(BF16)
    memb = mem.reshape(bsz * n_mem, D_MODEL).astype(BF16)
    chunk = 64
    for i in range(DEPTH):
        kind, j = i % N_MIXERS, i // N_MIXERS
        if kind == 0:
            w_in, n_main, n_gate = gdn_w_in[j], 4 * D_MODEL, 2 * GDN_HEADS
        elif kind == 1:
            w_in, n_main, n_gate = mlstm_w_in[j], 3 * D_MODEL, 2 * MLSTM_HEADS
        else:
            w_in, n_main, n_gate = fox_w_in[j], 4 * D_MODEL, FOX_HEADS
        w_main = w_in[:, :n_main].astype(BF16)
        w_gate = w_in[:, n_main:n_main + n_gate]
        w_memq = w_in[:, n_main + n_gate:].astype(BF16)
        proj = _matmul(xb, w_main, F32, 1024, 1024, "in_proj")
        kv = _matmul(memb, mem_w_kv[i].astype(BF16), BF16, 512, 1024, "mem_kv")
        memo = _mem_attention(xb, w_memq, kv.reshape(bsz, n_mem, 2 * MEM_W), bsz, seq, 512)
        if kind == 0:
            gcol, grow = _gates(xb, w_gate, gdn_a_log[j], gdn_dt_bias[j], "gdn", chunk, bsz, seq, 512)
            mix = _gdn_mixer(proj, gcol, grow, gdn_conv_w[j], gdn_norm_g[j], bsz, seq, 512, chunk)
        elif kind == 1:
            bias = jnp.concatenate([mlstm_b_gate[j, 0], mlstm_b_gate[j, 1]])
            gcol, grow = _gates(xb, w_gate, bias, bias, "mlstm", chunk, bsz, seq, 512)
            mix = _mlstm_mixer(proj, gcol, grow, mlstm_norm_g[j], bsz, seq, 512, chunk)
        else:
            _, crow = _gates(xb, w_gate, fox_b_f[j], fox_b_f[j], "fox", chunk, bsz, seq, 512)
            qn, kn, vb = _fox_prep(proj, fox_qk_g[j], 512)
            mix = _fox_attention(qn, kn, vb, crow, proj, bsz, seq, 512, 2)
        x, xb = _outproj_ln(mix.reshape(t, D_MODEL), memo, w_out[i].astype(BF16), x, ln1_g[i], ln1_b[i], 512)
        x, xb = _ffn_ln(xb, x, w_up[i].astype(BF16), w_down[i].astype(BF16), ln2_g[i], ln2_b[i], 512, 1024)
    return x.reshape(bsz, seq, D_MODEL)
```

```python
import functools
import math

import jax
import jax.numpy as jnp
from jax import lax
from jax.experimental import pallas as pl
from jax.experimental.pallas import tpu as pltpu

F32 = jnp.float32
BF16 = jnp.bfloat16

D_MODEL = 1024
DEPTH = 4
N_MIXERS = 3
GDN_HEADS = 8
GDN_D = 128
GDN_QKV = 3 * GDN_HEADS * GDN_D
CONV_K = 4
MLSTM_HEADS = 8
MLSTM_DQK = 64
MLSTM_DV = 128
FOX_HEADS = 16
FOX_DH = 64
MEM_HEADS = 4
MEM_DH = 128
MEM_W = MEM_HEADS * MEM_DH
D_FF = 4 * D_MODEL
ALPHA = (2 * DEPTH) ** 0.25
LN_EPS = 1e-5
NORM_EPS = 1e-6

LANES = 128
SUBLANES = 8
VMEM_LIMIT = 56 * 1024 * 1024

CHUNK = 64
N_GATE_ROWS = 16
N_SPLIT = 3
INV_BLOCK = 16
NEG_INF = float("-inf")
LOG2E = math.log2(math.e)


def _cparams(*sem):
    return pltpu.CompilerParams(dimension_semantics=sem, vmem_limit_bytes=VMEM_LIMIT)


def _dot(a, b):
    return jnp.dot(a.astype(BF16), b.astype(BF16), preferred_element_type=F32)


def _dot_nt(a, b):
    return lax.dot_general(a.astype(BF16), b.astype(BF16), (((1,), (1,)), ((), ())),
                           preferred_element_type=F32)


def _dot_tn(a, b):
    return lax.dot_general(a.astype(BF16), b.astype(BF16), (((0,), (0,)), ((), ())),
                           preferred_element_type=F32)


def _sigmoid(x):
    return 1.0 / (1.0 + jnp.exp(-x))


def _softplus(x):
    return jnp.maximum(x, 0.0) + jnp.log1p(jnp.exp(-jnp.abs(x)))


def _iota2(shape, dim):
    return lax.broadcasted_iota(jnp.int32, shape, dim)


def _mm_kernel(x_ref, w_ref, o_ref):
    o_ref[...] = jnp.dot(x_ref[...], w_ref[...], preferred_element_type=F32).astype(o_ref.dtype)


def _matmul(x, w, out_dtype, tm, tn, name):
    m, k = x.shape
    n = w.shape[1]
    tm, tn = min(tm, m), min(tn, n)
    assert m % tm == 0 and n % tn == 0
    return pl.pallas_call(
        _mm_kernel,
        out_shape=jax.ShapeDtypeStruct((m, n), out_dtype),
        grid=(m // tm, n // tn),
        in_specs=[pl.BlockSpec((tm, k), lambda i, j: (i, 0)),
                  pl.BlockSpec((k, tn), lambda i, j: (0, j))],
        out_specs=pl.BlockSpec((tm, tn), lambda i, j: (i, j)),
        compiler_params=_cparams("parallel", "parallel"),
        name=name)(x, w)


def _scan_rows(x, seg):
    pos = _iota2(x.shape, 0) & (seg - 1)
    d = 1
    while d < seg:
        x = x + jnp.where(pos >= d, pltpu.roll(x, d, 0), 0.0)
        d *= 2
    return x


def _scan_lanes(x, seg):
    pos = _iota2(x.shape, 1) & (seg - 1)
    d = 1
    while d < seg:
        x = x + jnp.where(pos >= d, pltpu.roll(x, d, 1), 0.0)
        d *= 2
    return x


def _gate_act(kind, y, p1, p2, idx):
    if kind == "gdn":
        g = -jnp.exp(p1) * _softplus(y + p2)
        val = jnp.where(idx < GDN_HEADS, g, _sigmoid(y))
        scan = idx < GDN_HEADS
    elif kind == "mlstm":
        z = y + p2
        val = jnp.where(idx < MLSTM_HEADS, z, -_softplus(-z))
        scan = idx >= MLSTM_HEADS
    else:
        val = -_softplus(-(y + p2))
        scan = idx >= 0
    return val, scan


def _gates_kernel(x_ref, wg_ref, wgt_ref, p1_ref, p2_ref, p1t_ref, p2t_ref, row_ref, *rest, kind, seg):
    full = kind == "fox"
    tg = x_ref.shape[0]
    x = x_ref[...]
    yrow = lax.dot_general(wgt_ref[...], x, (((1,), (1,)), ((), ())),
                           preferred_element_type=F32)
    vrow, srow = _gate_act(kind, yrow, p1t_ref[...], p2t_ref[...], _iota2(yrow.shape, 0))
    crow = _scan_lanes(vrow, tg if full else seg)
    if full:
        (carry_ref,) = rest

        @pl.when(pl.program_id(1) == 0)
        def _():
            carry_ref[...] = jnp.zeros_like(carry_ref)
        crow = crow + carry_ref[...]
        carry_ref[...] = crow[:, tg - 1:tg]
        row_ref[...] = crow[:N_GATE_ROWS, :]
        return
    split_ref, _ = rest
    row_ref[...] = jnp.where(srow, crow, vrow)[:N_GATE_ROWS, :]
    ycol = jnp.dot(x, wg_ref[...], preferred_element_type=F32)
    vcol, scol = _gate_act(kind, ycol, p1_ref[...], p2_ref[...], _iota2(ycol.shape, 1))
    col = jnp.where(scol, _scan_rows(vcol, seg), vcol)
    hi = col.astype(BF16)
    r1 = col - hi.astype(F32)
    mid = r1.astype(BF16)
    lo = (r1 - mid.astype(F32)).astype(BF16)
    split_ref[:, 0:LANES] = hi
    split_ref[:, LANES:2 * LANES] = mid
    split_ref[:, 2 * LANES:3 * LANES] = lo


def _gates(xb, wg, p1, p2, kind, bsz, seq, tg):
    ng = wg.shape[1]
    wgp = jnp.pad(wg, ((0, 0), (0, LANES - ng))).astype(BF16)
    p1p = jnp.pad(p1, (0, LANES - p1.shape[0])).astype(F32)
    p2p = jnp.pad(p2, (0, LANES - p2.shape[0])).astype(F32)
    tg = min(tg, seq)
    nblk = seq // tg
    t = bsz * seq
    kern = functools.partial(_gates_kernel, kind=kind, seg=CHUNK)
    const = lambda b, s: (0, 0)
    row_shape = jax.ShapeDtypeStruct((N_GATE_ROWS, t), F32)
    row_spec = pl.BlockSpec((N_GATE_ROWS, tg), lambda b, s: (0, b * nblk + s))
    if kind == "fox":
        out_shape, out_specs = row_shape, row_spec
    else:
        out_shape = (row_shape, jax.ShapeDtypeStruct((t, N_SPLIT * LANES), BF16))
        out_specs = (row_spec, pl.BlockSpec((tg, N_SPLIT * LANES), lambda b, s: (b * nblk + s, 0)))
    return pl.pallas_call(
        kern,
        out_shape=out_shape,
        grid=(bsz, nblk),
        in_specs=[pl.BlockSpec((tg, D_MODEL), lambda b, s: (b * nblk + s, 0)),
                  pl.BlockSpec((D_MODEL, LANES), const),
                  pl.BlockSpec((LANES, D_MODEL), const),
                  pl.BlockSpec((1, LANES), const), pl.BlockSpec((1, LANES), const),
                  pl.BlockSpec((LANES, 1), const), pl.BlockSpec((LANES, 1), const)],
        out_specs=out_specs,
        scratch_shapes=[pltpu.VMEM((LANES, 1), F32)],
        compiler_params=_cparams("parallel", "arbitrary"),
        name=f"gates_{kind}")(xb, wgp, wgp.T, p1p[None, :], p2p[None, :], p1p[:, None], p2p[:, None])


def _gate_selectors(n_heads):
    r = jnp.arange(N_SPLIT * LANES)[None, :, None] % LANES
    c = jnp.arange(2 * LANES)[None, None, :]
    h = jnp.arange(n_heads)[:, None, None]
    return jnp.where(c < LANES, r == h, r == n_heads + h).astype(BF16)


def _memattn_kernel(x_ref, wq_ref, kv_ref, o_ref):
    q = jnp.dot(x_ref[...], wq_ref[...], preferred_element_type=F32)
    scale = MEM_DH ** -0.5
    for h in range(MEM_HEADS):
        qh = q[:, h * MEM_DH:(h + 1) * MEM_DH]
        kh = kv_ref[0, :, h * MEM_DH:(h + 1) * MEM_DH]
        vh = kv_ref[0, :, MEM_W + h * MEM_DH:MEM_W + (h + 1) * MEM_DH]
        logits = _dot_nt(qh, kh) * scale
        mx = jnp.max(logits, axis=-1, keepdims=True)
        e = jnp.exp(logits - mx)
        p = e / jnp.sum(e, axis=-1, keepdims=True)
        o_ref[:, h * MEM_DH:(h + 1) * MEM_DH] = _dot(p, vh).astype(o_ref.dtype)


def _mem_attention(xb, wq, kv, bsz, seq, tm):
    tm = min(tm, seq)
    nblk = seq // tm
    n_mem = kv.shape[1]
    return pl.pallas_call(
        _memattn_kernel,
        out_shape=jax.ShapeDtypeStruct((bsz * seq, MEM_W), BF16),
        grid=(bsz, nblk),
        in_specs=[pl.BlockSpec((tm, D_MODEL), lambda b, s: (b * nblk + s, 0)),
                  pl.BlockSpec((D_MODEL, MEM_W), lambda b, s: (0, 0)),
                  pl.BlockSpec((1, n_mem, 2 * MEM_W), lambda b, s: (b, 0, 0))],
        out_specs=pl.BlockSpec((tm, MEM_W), lambda b, s: (b * nblk + s, 0)),
        compiler_params=_cparams("parallel", "parallel"),
        name="mem_attention")(xb, wq, kv)


def _unit_lower_inverse(a_list):
    c = a_list[0].shape[0]
    ri, ci = _iota2((c, c), 0), _iota2((c, c), 1)
    eye = (ri == ci).astype(F32)
    shift = int(math.log2(INV_BLOCK))
    same_block = (ri >> shift) == (ci >> shift)
    d = [jnp.where(same_block, a, 0.0) for a in a_list]
    n = [a - di for a, di in zip(a_list, d)]
    t = [eye - di for di in d]
    p = d
    for _ in range(shift - 1):
        p = [_dot(pi, pi) for pi in p]
        t = [ti + _dot(ti, pi) for ti, pi in zip(t, p)]
    nb = c // INV_BLOCK
    if nb > 1:
        m = [_dot(ti, ni) for ti, ni in zip(t, n)]
        x = [eye - mi for mi in m]
        p = m
        for _ in range(int(math.log2(nb)) - 1):
            p = [_dot(pi, pi) for pi in p]
            x = [xi + _dot(xi, pi) for xi, pi in zip(x, p)]
        t = [_dot(xi, ti) for xi, ti in zip(x, t)]
    return t


def _gdn_kernel(x_ref, gs_ref, grow_ref, sel_ref, cw_ref, ng_ref, o_ref,
                ext_ref, qkv_ref, u_ref, w_ref, qg_ref, kg_ref, attn_ref, dl_ref, st_ref, *, chunk, group):
    ts = x_ref.shape[1]
    nchunk = ts // chunk
    s_idx = pl.program_id(1)
    blk = pl.program_id(0) * pl.num_programs(1) + s_idx
    halo = SUBLANES
    nqk = GDN_HEADS * GDN_D

    @pl.when(s_idx == 0)
    def _():
        st_ref[...] = jnp.zeros_like(st_ref)
        ext_ref[0:halo, :] = jnp.zeros((halo, GDN_QKV), F32)

    ext_ref[halo:halo + ts, :] = x_ref[0, :, 0:GDN_QKV]
    for j in range(GDN_QKV // LANES):
        cols = slice(j * LANES, (j + 1) * LANES)
        acc = cw_ref[CONV_K - 1:CONV_K, cols] * ext_ref[halo:halo + ts, cols]
        for tap in range(CONV_K - 1):
            off = halo - (CONV_K - 1) + tap
            acc = acc + cw_ref[tap:tap + 1, cols] * ext_ref[off:off + ts, cols]
        y = acc * _sigmoid(acc)
        if j < 2 * GDN_HEADS:
            y = y * lax.rsqrt(jnp.sum(y * y, axis=-1, keepdims=True) + NORM_EPS)
        if j < GDN_HEADS:
            y = y * (GDN_D ** -0.5)
        qkv_ref[:, cols] = y
    ext_ref[0:halo, :] = ext_ref[ts:ts + halo, :]

    ri, ci = _iota2((chunk, chunk), 0), _iota2((chunk, chunk), 1)
    incl = ri >= ci
    strict = ri > ci
    ng = ng_ref[...]

    def intra(ci_, carry):
        items = [(cc, h) for cc in range(group) for h in range(GDN_HEADS)]
        cidx = [ci_ * group + cc for cc in range(group)]
        rows = [pl.ds(pl.multiple_of(c * chunk, chunk), chunk) for c in cidx]
        gsp = [gs_ref[r, :] for r in rows]
        gb = [_dot(gsp[cc], sel_ref[h]) for cc, h in items]
        gcc = [g[:, :LANES] for g in gb]
        bc = [g[:, LANES:] for g in gb]
        q = [qkv_ref[rows[cc], h * GDN_D:(h + 1) * GDN_D] for cc, h in items]
        k = [qkv_ref[rows[cc], nqk + h * GDN_D:nqk + (h + 1) * GDN_D] for cc, h in items]
        v = [qkv_ref[rows[cc], 2 * nqk + h * GDN_D:2 * nqk + (h + 1) * GDN_D] for cc, h in items]
        decay = [jnp.exp(jnp.where(incl, gcc[n][:, :chunk]
                                   - grow_ref[h, pl.ds(blk * nchunk + cidx[cc], 1), :], NEG_INF))
                 for n, (cc, h) in enumerate(items)]
        kb = [ki * bi for ki, bi in zip(k, bc)]
        kq = [_dot_nt(jnp.concatenate([kbi, qi], axis=0), ki) for kbi, qi, ki in zip(kb, q, k)]
        a = [jnp.where(strict, kqi[:chunk] * di, 0.0) for kqi, di in zip(kq, decay)]
        tinv = _unit_lower_inverse(a)
        egc = [jnp.exp(g) for g in gcc]
        sol = [_dot(ti, jnp.concatenate([vi * bi, kbi * ei], axis=1))
               for ti, vi, bi, kbi, ei in zip(tinv, v, bc, kb, egc)]
        for n, (cc, h) in enumerate(items):
            hc = slice(h * GDN_D, (h + 1) * GDN_D)
            glast = gcc[n][chunk - 1:chunk, :]
            u_ref[rows[cc], hc] = sol[n][:, :GDN_D]
            w_ref[rows[cc], hc] = sol[n][:, GDN_D:].astype(BF16)
            qg_ref[rows[cc], hc] = (q[n] * egc[n]).astype(BF16)
            kg_ref[rows[cc], hc] = (k[n] * jnp.exp(glast - gcc[n])).astype(BF16)
            attn_ref[h, rows[cc], :] = (kq[n][chunk:] * decay[n]).astype(BF16)
            dl_ref[h, pl.ds(cidx[cc], 1), :] = jnp.exp(glast)
        return carry

    lax.fori_loop(0, nchunk // group, intra, 0)

    def inter(c, carry):
        rows = pl.ds(pl.multiple_of(c * chunk, chunk), chunk)
        heads = range(GDN_HEADS)
        hcs = [slice(h * GDN_D, (h + 1) * GDN_D) for h in heads]
        state = [st_ref[h] for h in heads]
        wq = [_dot(jnp.concatenate([w_ref[rows, hcs[h]], qg_ref[rows, hcs[h]]], axis=0), state[h])
              for h in heads]
        v_new = [(u_ref[rows, hcs[h]] - wq[h][:chunk]).astype(BF16) for h in heads]
        o2 = [_dot(attn_ref[h, rows, :], v_new[h]) for h in heads]
        upd = [_dot_tn(kg_ref[rows, hcs[h]], v_new[h]) for h in heads]
        for h in heads:
            st_ref[h] = state[h] * dl_ref[h, pl.ds(c, 1), :] + upd[h]
        for h in heads:
            o = wq[h][chunk:] + o2[h]
            z = x_ref[0, rows, GDN_QKV + h * GDN_D:GDN_QKV + (h + 1) * GDN_D]
            on = o * lax.rsqrt(jnp.mean(o * o, axis=-1, keepdims=True) + NORM_EPS) * ng
            o_ref[0, rows, hcs[h]] = (on * (z * _sigmoid(z))).astype(o_ref.dtype)
        return carry

    lax.fori_loop(0, nchunk, inter, 0)


def _gdn_mixer(proj, gsplit, grow, conv_w, norm_g, bsz, seq, ts, group):
    ts = min(ts, seq)
    nblk = seq // ts
    t = bsz * seq
    ncols = proj.shape[1]
    nchunk = ts // CHUNK
    kern = functools.partial(_gdn_kernel, chunk=CHUNK, group=min(group, nchunk))
    grow3 = grow.reshape(grow.shape[0], t // CHUNK, CHUNK)
    sel = _gate_selectors(GDN_HEADS)
    full = lambda a: pl.BlockSpec(a.shape, lambda b, s: (0,) * a.ndim)
    return pl.pallas_call(
        kern,
        out_shape=jax.ShapeDtypeStruct((bsz, seq, D_MODEL), BF16),
        grid=(bsz, nblk),
        in_specs=[pl.BlockSpec((1, ts, ncols), lambda b, s: (b, s, 0)),
                  pl.BlockSpec((ts, N_SPLIT * LANES), lambda b, s: (b * nblk + s, 0)),
                  full(grow3), full(sel),
                  pl.BlockSpec((CONV_K, GDN_QKV), lambda b, s: (0, 0)),
                  pl.BlockSpec((1, GDN_D), lambda b, s: (0, 0))],
        out_specs=pl.BlockSpec((1, ts, D_MODEL), lambda b, s: (b, s, 0)),
        scratch_shapes=[pltpu.VMEM((ts + 2 * SUBLANES, GDN_QKV), F32),
                        pltpu.VMEM((ts, GDN_QKV), F32),
                        pltpu.VMEM((ts, D_MODEL), F32),
                        pltpu.VMEM((ts, D_MODEL), BF16),
                        pltpu.VMEM((ts, D_MODEL), BF16),
                        pltpu.VMEM((ts, D_MODEL), BF16),
                        pltpu.VMEM((GDN_HEADS, ts, CHUNK), BF16),
                        pltpu.VMEM((GDN_HEADS, nchunk, LANES), F32),
                        pltpu.VMEM((GDN_HEADS, GDN_D, GDN_D), F32)],
        compiler_params=_cparams("parallel", "arbitrary"),
        name="gdn_mixer")(proj.reshape(bsz, seq, ncols), gsplit, grow3, sel, conv_w, norm_g[None, :])


def _mlstm_kernel(x_ref, gs_ref, grow_ref, sel_ref, ng_ref, o_ref, st_ref, m_ref, *, chunk):
    ts = x_ref.shape[1]
    nchunk = ts // chunk
    s_idx = pl.program_id(1)
    blk = pl.program_id(0) * pl.num_programs(1) + s_idx
    nh, dv = MLSTM_HEADS, MLSTM_DV
    k0, v0, og0 = nh * MLSTM_DQK, 2 * nh * MLSTM_DQK, 2 * nh * MLSTM_DQK + nh * dv

    @pl.when(s_idx == 0)
    def _():
        st_ref[...] = jnp.zeros_like(st_ref)
        m_ref[...] = jnp.zeros_like(m_ref)

    incl = _iota2((chunk, chunk), 0) >= _iota2((chunk, chunk), 1)
    lo = _iota2((chunk, LANES), 1) < MLSTM_DQK
    ones = jnp.ones((chunk, LANES), F32)
    twice = lambda r: jnp.concatenate([r, r], axis=1)

    def body(c, carry):
        rows = pl.ds(pl.multiple_of(c * chunk, chunk), chunk)
        cg = blk * nchunk + c
        heads = range(nh)
        gsp = gs_ref[rows, :]
        gb = [_dot(gsp, sel_ref[h]) for h in heads]
        irep = [g[:, :LANES] for g in gb]
        brep = [g[:, LANES:] for g in gb]
        q2 = [x_ref[0, rows, p * LANES:(p + 1) * LANES] * (MLSTM_DQK ** -0.5) for p in range(nh // 2)]
        k2 = [x_ref[0, rows, k0 + p * LANES:k0 + (p + 1) * LANES] for p in range(nh // 2)]
        q = [jnp.where(lo if h % 2 == 0 else ~lo, q2[h // 2], 0.0) for h in heads]
        k = [jnp.where(lo if h % 2 == 0 else ~lo, k2[h // 2], 0.0) for h in heads]
        vext = [jnp.concatenate([x_ref[0, rows, v0 + h * dv:v0 + (h + 1) * dv], ones], axis=1)
                for h in heads]
        qk = [_dot_nt(q[h], k2[h // 2]) for h in heads]
        state = [st_ref[h] for h in heads]
        qs = [_dot(q[h], state[h]) for h in heads]
        logd = [jnp.where(incl, brep[h][:, :chunk] - grow_ref[nh + h, pl.ds(cg, 1), :]
                          + grow_ref[h, pl.ds(cg, 1), :], NEG_INF) for h in heads]
        m_intra = [jnp.max(ld, axis=-1, keepdims=True) for ld in logd]
        blast = [b[chunk - 1:chunk, :] for b in brep]
        lw = [blast[h] - brep[h] + irep[h] for h in heads]
        mw = [jnp.max(x, axis=0, keepdims=True) for x in lw]
        m_prev = [m_ref[h:h + 1, :] for h in heads]
        a_inter = [brep[h] + m_prev[h] for h in heads]
        m_t = [jnp.maximum(a_inter[h], m_intra[h]) for h in heads]
        p = [qk[h] * jnp.exp(logd[h] - m_t[h][:, :chunk]) for h in heads]
        pv = [_dot(p[h], vext[h]) for h in heads]
        m_new = [jnp.maximum(blast[h] + m_prev[h], mw[h]) for h in heads]
        for h in heads:
            kw = k[h] * jnp.exp(lw[h] - m_new[h])
            keep = jnp.exp(blast[h] + m_prev[h] - m_new[h])
            st_ref[h] = twice(keep) * state[h] + _dot_tn(kw, vext[h])
            m_ref[h:h + 1, :] = m_new[h]
        for h in heads:
            num = twice(jnp.exp(a_inter[h] - m_t[h])) * qs[h] + pv[h]
            hh = num[:, :dv] / jnp.maximum(jnp.abs(num[:, dv:]), jnp.exp(-m_t[h]))
            mu = jnp.mean(hh, axis=-1, keepdims=True)
            xc = hh - mu
            var = jnp.mean(xc * xc, axis=-1, keepdims=True)
            hn = xc * lax.rsqrt(var + NORM_EPS) * ng_ref[h:h + 1, :]
            og = x_ref[0, rows, og0 + h * dv:og0 + (h + 1) * dv]
            o_ref[0, rows, h * dv:(h + 1) * dv] = (hn * _sigmoid(og)).astype(o_ref.dtype)
        return carry

    lax.fori_loop(0, nchunk, body, 0)


def _mlstm_mixer(proj, gsplit, grow, norm_g, bsz, seq, ts):
    ts = min(ts, seq)
    nblk = seq // ts
    t = bsz * seq
    ncols = proj.shape[1]
    kern = functools.partial(_mlstm_kernel, chunk=CHUNK)
    grow3 = grow.reshape(grow.shape[0], t // CHUNK, CHUNK)
    sel = _gate_selectors(MLSTM_HEADS)
    full = lambda a: pl.BlockSpec(a.shape, lambda b, s: (0,) * a.ndim)
    return pl.pallas_call(
        kern,
        out_shape=jax.ShapeDtypeStruct((bsz, seq, D_MODEL), BF16),
        grid=(bsz, nblk),
        in_specs=[pl.BlockSpec((1, ts, ncols), lambda b, s: (b, s, 0)),
                  pl.BlockSpec((ts, N_SPLIT * LANES), lambda b, s: (b * nblk + s, 0)),
                  full(grow3), full(sel),
                  pl.BlockSpec((MLSTM_HEADS, MLSTM_DV), lambda b, s: (0, 0))],
        out_specs=pl.BlockSpec((1, ts, D_MODEL), lambda b, s: (b, s, 0)),
        scratch_shapes=[pltpu.VMEM((MLSTM_HEADS, LANES, 2 * LANES), F32),
                        pltpu.VMEM((MLSTM_HEADS, LANES), F32)],
        compiler_params=_cparams("parallel", "arbitrary"),
        name="mlstm_mixer")(proj.reshape(bsz, seq, ncols), gsplit, grow3, sel, norm_g)


def _fox_prep_kernel(q_ref, k_ref, v_ref, g_ref, qo_ref, ko_ref, vo_ref):
    half = _iota2((q_ref.shape[0], LANES), 1) < FOX_DH

    def norm(x_ref, o_ref, g_row, scale):
        for j in range(x_ref.shape[1] // LANES):
            cols = slice(j * LANES, (j + 1) * LANES)
            x = x_ref[:, cols]
            sq = x * x
            s_lo = jnp.sum(jnp.where(half, sq, 0.0), axis=-1, keepdims=True)
            s_all = jnp.sum(sq, axis=-1, keepdims=True)
            ms = jnp.where(half, s_lo, s_all - s_lo) * (1.0 / FOX_DH)
            y = x * lax.rsqrt(ms + NORM_EPS) * g_ref[g_row:g_row + 1, cols]
            o_ref[:, cols] = (y * scale).astype(o_ref.dtype)

    norm(q_ref, qo_ref, 0, FOX_DH ** -0.5 * LOG2E)
    norm(k_ref, ko_ref, 1, 1.0)
    vo_ref[...] = v_ref[...].astype(vo_ref.dtype)


def _fox_prep(proj, qk_g, tp):
    t = proj.shape[0]
    tp = min(tp, t)
    g = jnp.tile(qk_g, (1, FOX_HEADS))
    spec = lambda j: pl.BlockSpec((tp, D_MODEL), lambda i: (i, j))
    out = jax.ShapeDtypeStruct((t, D_MODEL), BF16)
    return pl.pallas_call(
        _fox_prep_kernel,
        out_shape=(out, out, out),
        grid=(t // tp,),
        in_specs=[spec(0), spec(1), spec(2), pl.BlockSpec((2, D_MODEL), lambda i: (0, 0))],
        out_specs=(spec(0), spec(0), spec(0)),
        compiler_params=_cparams("parallel"),
        name="fox_prep")(proj, proj, proj, g)


def _fox_kernel(q_ref, k_ref, v_ref, crow_ref, og_ref, o_ref, m_ref, acc_ref, *, tq, wide):
    i = pl.program_id(2)
    hp = pl.program_id(1)
    pair = range(2)
    lo = _iota2((tq, LANES), 1) < FOX_DH
    q = q_ref[0]
    zero = jnp.zeros_like(q)
    qs = (jnp.where(lo, q, zero), jnp.where(lo, zero, q))
    q0 = pl.multiple_of(i * tq, tq)
    cref = [crow_ref[2 * hp + e, :, pl.ds(q0, LANES)][:, 0:1] for e in pair]
    m_ref[...] = jnp.full(m_ref.shape, NEG_INF, F32)
    acc_ref[...] = jnp.zeros_like(acc_ref)
    causal = _iota2((tq, tq), 0) >= _iota2((tq, tq), 1)

    def step(k0, width, masked):
        k = k_ref[0, pl.ds(k0, width), :]
        vext = jnp.concatenate([v_ref[0, pl.ds(k0, width), :], jnp.ones((width, LANES), BF16)], axis=1)
        s = [lax.dot_general(qs[e], k, (((1,), (1,)), ((), ())), preferred_element_type=F32) for e in pair]
        for e in pair:
            bias = (cref[e] - crow_ref[2 * hp + e, :, pl.ds(k0, width)]) * LOG2E
            s[e] = s[e] + bias
            if masked:
                s[e] = jnp.where(causal, s[e], NEG_INF)
        m_prev = [m_ref[e] for e in pair]
        m_new = [jnp.maximum(m_prev[e], jnp.max(s[e], axis=-1, keepdims=True)) for e in pair]
        p = [jnp.exp2(s[e] - jnp.concatenate([m_new[e]] * (width // LANES), axis=1)) for e in pair]
        pv = [jnp.dot(p[e].astype(BF16), vext, preferred_element_type=F32) for e in pair]
        for e in pair:
            alpha = jnp.exp2(m_prev[e] - m_new[e])
            acc_ref[e] = jnp.concatenate([alpha, alpha], axis=1) * acc_ref[e] + pv[e]
            m_ref[e] = m_new[e]

    def body(j, carry):
        step(pl.multiple_of(j * (wide * tq), wide * tq), wide * tq, False)
        return carry

    lax.fori_loop(0, i // wide, body, 0)
    if wide == 2:
        @pl.when(i % 2 == 1)
        def _():
            step(pl.multiple_of((i - 1) * tq, tq), tq, False)
    step(q0, tq, True)
    out = jnp.where(lo, acc_ref[0, :, :LANES] / acc_ref[0, :, LANES:], acc_ref[1, :, :LANES] / acc_ref[1, :, LANES:])
    o_ref[0] = (out * _sigmoid(og_ref[0])).astype(o_ref.dtype)


def _fox_attention(qn, kn, vb, crow, proj, bsz, seq, tq, wide):
    tq = min(tq, seq)
    nq = seq // tq
    npair = FOX_HEADS // 2
    kern = functools.partial(_fox_kernel, tq=tq, wide=wide)
    og_blk = 3 * D_MODEL // LANES
    return pl.pallas_call(
        kern,
        out_shape=jax.ShapeDtypeStruct((bsz, seq, D_MODEL), BF16),
        grid=(bsz, npair, nq),
        in_specs=[pl.BlockSpec((1, tq, LANES), lambda b, h, i: (b, i, h)),
                  pl.BlockSpec((1, seq, LANES), lambda b, h, i: (b, 0, h)),
                  pl.BlockSpec((1, seq, LANES), lambda b, h, i: (b, 0, h)),
                  pl.BlockSpec((FOX_HEADS, 1, seq), lambda b, h, i: (0, 0, b)),
                  pl.BlockSpec((1, tq, LANES), lambda b, h, i: (b, i, og_blk + h))],
        out_specs=pl.BlockSpec((1, tq, LANES), lambda b, h, i: (b, i, h)),
        scratch_shapes=[pltpu.VMEM((2, tq, LANES), F32), pltpu.VMEM((2, tq, 2 * LANES), F32)],
        compiler_params=_cparams("parallel", "parallel", "arbitrary"),
        name="fox_attention")(qn.reshape(bsz, seq, D_MODEL), kn.reshape(bsz, seq, D_MODEL),
                              vb.reshape(bsz, seq, D_MODEL), crow.reshape(FOX_HEADS, 1, bsz * seq),
                              proj.reshape(bsz, seq, proj.shape[1]))


def _layer_norm(z, g, b):
    mu = jnp.mean(z, axis=-1, keepdims=True)
    zc = z - mu
    var = jnp.mean(zc * zc, axis=-1, keepdims=True)
    return zc * lax.rsqrt(var + LN_EPS) * g + b


def _outproj_kernel(mix_ref, mem_ref, w_ref, x_ref, g_ref, b_ref, o_ref, ob_ref):
    y = jnp.dot(mix_ref[...], w_ref[0:D_MODEL, :], preferred_element_type=F32)
    y = y + jnp.dot(mem_ref[...], w_ref[D_MODEL:D_MODEL + MEM_W, :], preferred_element_type=F32)
    out = _layer_norm(ALPHA * x_ref[...] + y, g_ref[...], b_ref[...])
    o_ref[...] = out
    ob_ref[...] = out.astype(ob_ref.dtype)


def _outproj_ln(mix, memo, w, x, g, b, tm):
    t = x.shape[0]
    tm = min(tm, t)
    row = lambda n: pl.BlockSpec((tm, n), lambda i: (i, 0))
    const = lambda s: pl.BlockSpec(s, lambda i: (0, 0))
    return pl.pallas_call(
        _outproj_kernel,
        out_shape=(jax.ShapeDtypeStruct((t, D_MODEL), F32), jax.ShapeDtypeStruct((t, D_MODEL), BF16)),
        grid=(t // tm,),
        in_specs=[row(D_MODEL), row(MEM_W), const((D_MODEL + MEM_W, D_MODEL)), row(D_MODEL),
                  const((1, D_MODEL)), const((1, D_MODEL))],
        out_specs=(row(D_MODEL), row(D_MODEL)),
        compiler_params=_cparams("parallel"),
        name="outproj_ln")(mix, memo, w, x, g[None, :], b[None, :])


def _ffn_kernel(xb_ref, x_ref, wu_ref, wd_ref, g_ref, b_ref, o_ref, ob_ref, *, fchunk):
    xb = xb_ref[...]
    y = jnp.zeros(x_ref.shape, F32)
    for c in range(D_FF // fchunk):
        h = jnp.dot(xb, wu_ref[:, c * fchunk:(c + 1) * fchunk], preferred_element_type=F32)
        h = jnp.square(jnp.maximum(h, 0.0)).astype(BF16)
        y = y + jnp.dot(h, wd_ref[c * fchunk:(c + 1) * fchunk, :], preferred_element_type=F32)
    out = _layer_norm(ALPHA * x_ref[...] + y, g_ref[...], b_ref[...])
    o_ref[...] = out
    ob_ref[...] = out.astype(ob_ref.dtype)


def _ffn_ln(xb, x, wu, wd, g, b, tm, fchunk):
    t = x.shape[0]
    tm = min(tm, t)
    row = pl.BlockSpec((tm, D_MODEL), lambda i: (i, 0))
    const = lambda s: pl.BlockSpec(s, lambda i: (0, 0), pipeline_mode=pl.Buffered(1))
    kern = functools.partial(_ffn_kernel, fchunk=fchunk)
    return pl.pallas_call(
        kern,
        out_shape=(jax.ShapeDtypeStruct((t, D_MODEL), F32), jax.ShapeDtypeStruct((t, D_MODEL), BF16)),
        grid=(t // tm,),
        in_specs=[row, row, const((D_MODEL, D_FF)), const((D_FF, D_MODEL)),
                  const((1, D_MODEL)), const((1, D_MODEL))],
        out_specs=(row, row),
        compiler_params=_cparams("parallel"),
        name="ffn_ln")(xb, x, wu, wd, g[None, :], b[None, :])


def kernel(x, mem, gdn_w_in, gdn_conv_w, gdn_a_log, gdn_dt_bias, gdn_norm_g, mlstm_w_in, mlstm_b_gate,
           mlstm_norm_g, fox_w_in, fox_b_f, fox_qk_g, mem_w_kv, w_out, ln1_g, ln1_b, w_up, w_down,
           ln2_g, ln2_b):
    bsz, seq, _ = x.shape
    t = bsz * seq
    n_mem = mem.shape[1]
    x = x.reshape(t, D_MODEL)
    xb = x.astype(BF16)
    memb = mem.reshape(bsz * n_mem, D_MODEL).astype(BF16)
    for i in range(DEPTH):
        kind, j = i % N_MIXERS, i // N_MIXERS
        if kind == 0:
            w_in, n_main, n_gate = gdn_w_in[j], 4 * D_MODEL, 2 * GDN_HEADS
        elif kind == 1:
            w_in, n_main, n_gate = mlstm_w_in[j], 3 * D_MODEL, 2 * MLSTM_HEADS
        else:
            w_in, n_main, n_gate = fox_w_in[j], 4 * D_MODEL, FOX_HEADS
        w_main = w_in[:, :n_main].astype(BF16)
        w_gate = w_in[:, n_main:n_main + n_gate]
        w_memq = w_in[:, n_main + n_gate:].astype(BF16)
        proj = _matmul(xb, w_main, F32, 1024, 1024, "in_proj")
        kv = _matmul(memb, mem_w_kv[i].astype(BF16), BF16, 512, 1024, "mem_kv")
        memo = _mem_attention(xb, w_memq, kv.reshape(bsz, n_mem, 2 * MEM_W), bsz, seq, 512)
        if kind == 0:
            grow, gsplit = _gates(xb, w_gate, gdn_a_log[j], gdn_dt_bias[j], "gdn", bsz, seq, 512)
            mix = _gdn_mixer(proj, gsplit, grow, gdn_conv_w[j], gdn_norm_g[j], bsz, seq, 512, 4)
        elif kind == 1:
            bias = jnp.concatenate([mlstm_b_gate[j, 0], mlstm_b_gate[j, 1]])
            grow, gsplit = _gates(xb, w_gate, bias, bias, "mlstm", bsz, seq, 512)
            mix = _mlstm_mixer(proj, gsplit, grow, mlstm_norm_g[j], bsz, seq, 512)
        else:
            crow = _gates(xb, w_gate, fox_b_f[j], fox_b_f[j], "fox", bsz, seq, 512)
            qn, kn, vb = _fox_prep(proj, fox_qk_g[j], 512)
            mix = _fox_attention(qn, kn, vb, crow, proj, bsz, seq, 512, 2)
        x, xb = _outproj_ln(mix.reshape(t, D_MODEL), memo, w_out[i].astype(BF16), x, ln1_g[i], ln1_b[i], 512)
        x, xb = _ffn_ln(xb, x, w_up[i].astype(BF16), w_down[i].astype(BF16), ln2_g[i], ln2_b[i], 512, 1024)
    return x.reshape(bsz, seq, D_MODEL)
```

```python
import functools
import math

import jax
import jax.numpy as jnp
from jax import lax
from jax.experimental import pallas as pl
from jax.experimental.pallas import tpu as pltpu

F32 = jnp.float32
BF16 = jnp.bfloat16

D_MODEL = 1024
DEPTH = 4
N_MIXERS = 3
GDN_HEADS = 8
GDN_D = 128
GDN_QKV = 3 * GDN_HEADS * GDN_D
CONV_K = 4
MLSTM_HEADS = 8
MLSTM_DQK = 64
MLSTM_DV = 128
FOX_HEADS = 16
FOX_DH = 64
MEM_HEADS = 4
MEM_DH = 128
MEM_W = MEM_HEADS * MEM_DH
D_FF = 4 * D_MODEL
ALPHA = (2 * DEPTH) ** 0.25
LN_EPS = 1e-5
NORM_EPS = 1e-6

LANES = 128
SUBLANES = 8
VMEM_LIMIT = 56 * 1024 * 1024

CHUNK = 64
N_GATE_ROWS = 16
N_SPLIT = 3
INV_BLOCK = 16
NEG_INF = float("-inf")
LOG2E = math.log2(math.e)
FOX_UNDERFLOW = 160.0


def _cparams(*sem):
    return pltpu.CompilerParams(dimension_semantics=sem, vmem_limit_bytes=VMEM_LIMIT)


def _dot(a, b):
    return jnp.dot(a.astype(BF16), b.astype(BF16), preferred_element_type=F32)


def _dot_nt(a, b):
    return lax.dot_general(a.astype(BF16), b.astype(BF16), (((1,), (1,)), ((), ())),
                           preferred_element_type=F32)


def _dot_tn(a, b):
    return lax.dot_general(a.astype(BF16), b.astype(BF16), (((0,), (0,)), ((), ())),
                           preferred_element_type=F32)


def _sigmoid(x):
    return 1.0 / (1.0 + jnp.exp(-x))


def _softplus(x):
    return jnp.maximum(x, 0.0) + jnp.log1p(jnp.exp(-jnp.abs(x)))


def _iota2(shape, dim):
    return lax.broadcasted_iota(jnp.int32, shape, dim)


def _mm_kernel(x_ref, w_ref, o_ref):
    o_ref[...] = jnp.dot(x_ref[...], w_ref[...], preferred_element_type=F32).astype(o_ref.dtype)


def _matmul(x, w, out_dtype, tm, tn, name):
    m, k = x.shape
    n = w.shape[1]
    tm, tn = min(tm, m), min(tn, n)
    assert m % tm == 0 and n % tn == 0
    return pl.pallas_call(
        _mm_kernel,
        out_shape=jax.ShapeDtypeStruct((m, n), out_dtype),
        grid=(m // tm, n // tn),
        in_specs=[pl.BlockSpec((tm, k), lambda i, j: (i, 0)),
                  pl.BlockSpec((k, tn), lambda i, j: (0, j))],
        out_specs=pl.BlockSpec((tm, tn), lambda i, j: (i, j)),
        compiler_params=_cparams("parallel", "parallel"),
        name=name)(x, w)


def _scan_rows(x, seg):
    pos = _iota2(x.shape, 0) & (seg - 1)
    d = 1
    while d < seg:
        x = x + jnp.where(pos >= d, pltpu.roll(x, d, 0), 0.0)
        d *= 2
    return x


def _scan_lanes(x, seg):
    pos = _iota2(x.shape, 1) & (seg - 1)
    d = 1
    while d < seg:
        x = x + jnp.where(pos >= d, pltpu.roll(x, d, 1), 0.0)
        d *= 2
    return x


def _gate_act(kind, y, p1, p2, idx):
    if kind == "gdn":
        g = -jnp.exp(p1) * _softplus(y + p2)
        val = jnp.where(idx < GDN_HEADS, g, _sigmoid(y))
        scan = idx < GDN_HEADS
    elif kind == "mlstm":
        z = y + p2
        val = jnp.where(idx < MLSTM_HEADS, z, -_softplus(-z))
        scan = idx >= MLSTM_HEADS
    else:
        val = -_softplus(-(y + p2))
        scan = idx >= 0
    return val, scan


def _gates_kernel(x_ref, wg_ref, wgt_ref, p1_ref, p2_ref, p1t_ref, p2t_ref, row_ref, *rest, kind, seg):
    full = kind == "fox"
    tg = x_ref.shape[0]
    x = x_ref[...]
    yrow = lax.dot_general(wgt_ref[...], x, (((1,), (1,)), ((), ())),
                           preferred_element_type=F32)
    vrow, srow = _gate_act(kind, yrow, p1t_ref[...], p2t_ref[...], _iota2(yrow.shape, 0))
    crow = _scan_lanes(vrow, tg if full else seg)
    if full:
        (carry_ref,) = rest

        @pl.when(pl.program_id(1) == 0)
        def _():
            carry_ref[...] = jnp.zeros_like(carry_ref)
        crow = crow + carry_ref[...]
        carry_ref[...] = crow[:, tg - 1:tg]
        row_ref[...] = crow[:N_GATE_ROWS, :]
        return
    split_ref, _ = rest
    row_ref[...] = jnp.where(srow, crow, vrow)[:N_GATE_ROWS, :]
    ycol = jnp.dot(x, wg_ref[...], preferred_element_type=F32)
    vcol, scol = _gate_act(kind, ycol, p1_ref[...], p2_ref[...], _iota2(ycol.shape, 1))
    col = jnp.where(scol, _scan_rows(vcol, seg), vcol)
    hi = col.astype(BF16)
    r1 = col - hi.astype(F32)
    mid = r1.astype(BF16)
    lo = (r1 - mid.astype(F32)).astype(BF16)
    split_ref[:, 0:LANES] = hi
    split_ref[:, LANES:2 * LANES] = mid
    split_ref[:, 2 * LANES:3 * LANES] = lo


def _gates(xb, wg, p1, p2, kind, bsz, seq, tg):
    ng = wg.shape[1]
    wgp = jnp.pad(wg, ((0, 0), (0, LANES - ng))).astype(BF16)
    p1p = jnp.pad(p1, (0, LANES - p1.shape[0])).astype(F32)
    p2p = jnp.pad(p2, (0, LANES - p2.shape[0])).astype(F32)
    tg = min(tg, seq)
    nblk = seq // tg
    t = bsz * seq
    kern = functools.partial(_gates_kernel, kind=kind, seg=CHUNK)
    const = lambda b, s: (0, 0)
    row_shape = jax.ShapeDtypeStruct((N_GATE_ROWS, t), F32)
    row_spec = pl.BlockSpec((N_GATE_ROWS, tg), lambda b, s: (0, b * nblk + s))
    if kind == "fox":
        out_shape, out_specs = row_shape, row_spec
    else:
        out_shape = (row_shape, jax.ShapeDtypeStruct((t, N_SPLIT * LANES), BF16))
        out_specs = (row_spec, pl.BlockSpec((tg, N_SPLIT * LANES), lambda b, s: (b * nblk + s, 0)))
    return pl.pallas_call(
        kern,
        out_shape=out_shape,
        grid=(bsz, nblk),
        in_specs=[pl.BlockSpec((tg, D_MODEL), lambda b, s: (b * nblk + s, 0)),
                  pl.BlockSpec((D_MODEL, LANES), const),
                  pl.BlockSpec((LANES, D_MODEL), const),
                  pl.BlockSpec((1, LANES), const), pl.BlockSpec((1, LANES), const),
                  pl.BlockSpec((LANES, 1), const), pl.BlockSpec((LANES, 1), const)],
        out_specs=out_specs,
        scratch_shapes=[pltpu.VMEM((LANES, 1), F32)],
        compiler_params=_cparams("parallel", "arbitrary"),
        name=f"gates_{kind}")(xb, wgp, wgp.T, p1p[None, :], p2p[None, :], p1p[:, None], p2p[:, None])


def _gate_selectors(n_heads):
    r = jnp.arange(N_SPLIT * LANES)[None, :, None] % LANES
    c = jnp.arange(2 * LANES)[None, None, :]
    h = jnp.arange(n_heads)[:, None, None]
    return jnp.where(c < LANES, r == h, r == n_heads + h).astype(BF16)


def _memattn_kernel(x_ref, wq_ref, kv_ref, o_ref):
    q = jnp.dot(x_ref[...], wq_ref[...], preferred_element_type=F32)
    scale = MEM_DH ** -0.5
    for h in range(MEM_HEADS):
        qh = q[:, h * MEM_DH:(h + 1) * MEM_DH]
        kh = kv_ref[0, :, h * MEM_DH:(h + 1) * MEM_DH]
        vh = kv_ref[0, :, MEM_W + h * MEM_DH:MEM_W + (h + 1) * MEM_DH]
        logits = _dot_nt(qh, kh) * scale
        mx = jnp.max(logits, axis=-1, keepdims=True)
        e = jnp.exp(logits - mx)
        p = e / jnp.sum(e, axis=-1, keepdims=True)
        o_ref[:, h * MEM_DH:(h + 1) * MEM_DH] = _dot(p, vh).astype(o_ref.dtype)


def _mem_attention(xb, wq, kv, bsz, seq, tm):
    tm = min(tm, seq)
    nblk = seq // tm
    n_mem = kv.shape[1]
    return pl.pallas_call(
        _memattn_kernel,
        out_shape=jax.ShapeDtypeStruct((bsz * seq, MEM_W), BF16),
        grid=(bsz, nblk),
        in_specs=[pl.BlockSpec((tm, D_MODEL), lambda b, s: (b * nblk + s, 0)),
                  pl.BlockSpec((D_MODEL, MEM_W), lambda b, s: (0, 0)),
                  pl.BlockSpec((1, n_mem, 2 * MEM_W), lambda b, s: (b, 0, 0))],
        out_specs=pl.BlockSpec((tm, MEM_W), lambda b, s: (b * nblk + s, 0)),
        compiler_params=_cparams("parallel", "parallel"),
        name="mem_attention")(xb, wq, kv)


def _unit_lower_inverse(a_list):
    c = a_list[0].shape[0]
    ri, ci = _iota2((c, c), 0), _iota2((c, c), 1)
    eye = (ri == ci).astype(F32)
    shift = int(math.log2(INV_BLOCK))
    same_block = (ri >> shift) == (ci >> shift)
    d = [jnp.where(same_block, a, 0.0) for a in a_list]
    n = [a - di for a, di in zip(a_list, d)]
    t = [eye - di for di in d]
    p = d
    for _ in range(shift - 1):
        p = [_dot(pi, pi) for pi in p]
        t = [ti + _dot(ti, pi) for ti, pi in zip(t, p)]
    nb = c // INV_BLOCK
    if nb > 1:
        m = [_dot(ti, ni) for ti, ni in zip(t, n)]
        x = [eye - mi for mi in m]
        p = m
        for _ in range(int(math.log2(nb)) - 1):
            p = [_dot(pi, pi) for pi in p]
            x = [xi + _dot(xi, pi) for xi, pi in zip(x, p)]
        t = [_dot(xi, ti) for xi, ti in zip(x, t)]
    return t


def _gdn_kernel(x_ref, z_ref, gs_ref, grow_ref, sel_ref, cw_ref, ng_ref, o_ref,
                ext_ref, qkv_ref, u_ref, w_ref, qg_ref, kg_ref, attn_ref, dl_ref, st_ref, *, chunk, group):
    ts = x_ref.shape[1]
    nchunk = ts // chunk
    s_idx = pl.program_id(1)
    blk = pl.program_id(0) * pl.num_programs(1) + s_idx
    halo = SUBLANES
    nqk = GDN_HEADS * GDN_D

    @pl.when(s_idx == 0)
    def _():
        st_ref[...] = jnp.zeros_like(st_ref)
        ext_ref[0:halo, :] = jnp.zeros((halo, GDN_QKV), F32)

    ext_ref[halo:halo + ts, :] = x_ref[0, :, 0:GDN_QKV]
    for j in range(GDN_QKV // LANES):
        cols = slice(j * LANES, (j + 1) * LANES)
        acc = cw_ref[CONV_K - 1:CONV_K, cols] * ext_ref[halo:halo + ts, cols]
        for tap in range(CONV_K - 1):
            off = halo - (CONV_K - 1) + tap
            acc = acc + cw_ref[tap:tap + 1, cols] * ext_ref[off:off + ts, cols]
        y = acc * _sigmoid(acc)
        if j < 2 * GDN_HEADS:
            y = y * lax.rsqrt(jnp.sum(y * y, axis=-1, keepdims=True) + NORM_EPS)
        if j < GDN_HEADS:
            y = y * (GDN_D ** -0.5)
        qkv_ref[:, cols] = y
    ext_ref[0:halo, :] = ext_ref[ts:ts + halo, :]

    ri, ci = _iota2((chunk, chunk), 0), _iota2((chunk, chunk), 1)
    incl = ri >= ci
    strict = ri > ci
    ng = ng_ref[...]

    def intra(ci_, carry):
        items = [(cc, h) for cc in range(group) for h in range(GDN_HEADS)]
        cidx = [ci_ * group + cc for cc in range(group)]
        rows = [pl.ds(pl.multiple_of(c * chunk, chunk), chunk) for c in cidx]
        gsp = [gs_ref[r, :] for r in rows]
        gb = [_dot(gsp[cc], sel_ref[h]) for cc, h in items]
        gcc = [g[:, :LANES] for g in gb]
        bc = [g[:, LANES:] for g in gb]
        q = [qkv_ref[rows[cc], h * GDN_D:(h + 1) * GDN_D] for cc, h in items]
        k = [qkv_ref[rows[cc], nqk + h * GDN_D:nqk + (h + 1) * GDN_D] for cc, h in items]
        v = [qkv_ref[rows[cc], 2 * nqk + h * GDN_D:2 * nqk + (h + 1) * GDN_D] for cc, h in items]
        decay = [jnp.exp(jnp.where(incl, gcc[n][:, :chunk]
                                   - grow_ref[h, pl.ds(blk * nchunk + cidx[cc], 1), :], NEG_INF))
                 for n, (cc, h) in enumerate(items)]
        kb = [ki * bi for ki, bi in zip(k, bc)]
        kq = [_dot_nt(jnp.concatenate([kbi, qi], axis=0), ki) for kbi, qi, ki in zip(kb, q, k)]
        a = [jnp.where(strict, kqi[:chunk] * di, 0.0) for kqi, di in zip(kq, decay)]
        tinv = _unit_lower_inverse(a)
        egc = [jnp.exp(g) for g in gcc]
        sol = [_dot(ti, jnp.concatenate([vi * bi, kbi * ei], axis=1))
               for ti, vi, bi, kbi, ei in zip(tinv, v, bc, kb, egc)]
        for n, (cc, h) in enumerate(items):
            hc = slice(h * GDN_D, (h + 1) * GDN_D)
            glast = gcc[n][chunk - 1:chunk, :]
            u_ref[rows[cc], hc] = sol[n][:, :GDN_D]
            w_ref[rows[cc], hc] = sol[n][:, GDN_D:].astype(BF16)
            qg_ref[rows[cc], hc] = (q[n] * egc[n]).astype(BF16)
            kg_ref[rows[cc], hc] = (k[n] * jnp.exp(glast - gcc[n])).astype(BF16)
            attn_ref[h, rows[cc], :] = (kq[n][chunk:] * decay[n]).astype(BF16)
            dl_ref[h, pl.ds(cidx[cc], 1), :] = jnp.exp(glast)
        return carry

    lax.fori_loop(0, nchunk // group, intra, 0)

    def inter(c, carry):
        rows = pl.ds(pl.multiple_of(c * chunk, chunk), chunk)
        heads = range(GDN_HEADS)
        hcs = [slice(h * GDN_D, (h + 1) * GDN_D) for h in heads]
        state = [st_ref[h] for h in heads]
        wq = [_dot(jnp.concatenate([w_ref[rows, hcs[h]], qg_ref[rows, hcs[h]]], axis=0), state[h])
              for h in heads]
        v_new = [(u_ref[rows, hcs[h]] - wq[h][:chunk]).astype(BF16) for h in heads]
        o2 = [_dot(attn_ref[h, rows, :], v_new[h]) for h in heads]
        upd = [_dot_tn(kg_ref[rows, hcs[h]], v_new[h]) for h in heads]
        for h in heads:
            st_ref[h] = state[h] * dl_ref[h, pl.ds(c, 1), :] + upd[h]
        for h in heads:
            o = wq[h][chunk:] + o2[h]
            z = z_ref[0, rows, hcs[h]].astype(F32)
            on = o * lax.rsqrt(jnp.mean(o * o, axis=-1, keepdims=True) + NORM_EPS) * ng
            o_ref[0, rows, hcs[h]] = (on * (z * _sigmoid(z))).astype(o_ref.dtype)
        return carry

    lax.fori_loop(0, nchunk, inter, 0)


def _gdn_mixer(proj, zproj, gsplit, grow, conv_w, norm_g, bsz, seq, ts, group):
    ts = min(ts, seq)
    nblk = seq // ts
    t = bsz * seq
    ncols = proj.shape[1]
    assert ncols == GDN_QKV
    nchunk = ts // CHUNK
    kern = functools.partial(_gdn_kernel, chunk=CHUNK, group=min(group, nchunk))
    grow3 = grow.reshape(grow.shape[0], t // CHUNK, CHUNK)
    sel = _gate_selectors(GDN_HEADS)
    full = lambda a: pl.BlockSpec(a.shape, lambda b, s: (0,) * a.ndim)
    return pl.pallas_call(
        kern,
        out_shape=jax.ShapeDtypeStruct((bsz, seq, D_MODEL), BF16),
        grid=(bsz, nblk),
        in_specs=[pl.BlockSpec((1, ts, ncols), lambda b, s: (b, s, 0)),
                  pl.BlockSpec((1, ts, D_MODEL), lambda b, s: (b, s, 0)),
                  pl.BlockSpec((ts, N_SPLIT * LANES), lambda b, s: (b * nblk + s, 0)),
                  full(grow3), full(sel),
                  pl.BlockSpec((CONV_K, GDN_QKV), lambda b, s: (0, 0)),
                  pl.BlockSpec((1, GDN_D), lambda b, s: (0, 0))],
        out_specs=pl.BlockSpec((1, ts, D_MODEL), lambda b, s: (b, s, 0)),
        scratch_shapes=[pltpu.VMEM((ts + 2 * SUBLANES, GDN_QKV), F32),
                        pltpu.VMEM((ts, GDN_QKV), F32),
                        pltpu.VMEM((ts, D_MODEL), F32),
                        pltpu.VMEM((ts, D_MODEL), BF16),
                        pltpu.VMEM((ts, D_MODEL), BF16),
                        pltpu.VMEM((ts, D_MODEL), BF16),
                        pltpu.VMEM((GDN_HEADS, ts, CHUNK), BF16),
                        pltpu.VMEM((GDN_HEADS, nchunk, LANES), F32),
                        pltpu.VMEM((GDN_HEADS, GDN_D, GDN_D), F32)],
        compiler_params=_cparams("parallel", "arbitrary"),
        name="gdn_mixer")(proj.reshape(bsz, seq, ncols), zproj.reshape(bsz, seq, D_MODEL), gsplit, grow3,
                          sel, conv_w, norm_g[None, :])


def _mlstm_kernel(x_ref, gs_ref, grow_ref, sel_ref, ng_ref, o_ref, st_ref, m_ref, *, chunk):
    ts = x_ref.shape[1]
    nchunk = ts // chunk
    s_idx = pl.program_id(1)
    blk = pl.program_id(0) * pl.num_programs(1) + s_idx
    nh, dv = MLSTM_HEADS, MLSTM_DV
    k0, v0, og0 = nh * MLSTM_DQK, 2 * nh * MLSTM_DQK, 2 * nh * MLSTM_DQK + nh * dv

    @pl.when(s_idx == 0)
    def _():
        st_ref[...] = jnp.zeros_like(st_ref)
        m_ref[...] = jnp.zeros_like(m_ref)

    incl = _iota2((chunk, chunk), 0) >= _iota2((chunk, chunk), 1)
    lo = _iota2((chunk, LANES), 1) < MLSTM_DQK
    ones = jnp.ones((chunk, LANES), x_ref.dtype)
    twice = lambda r: jnp.concatenate([r, r], axis=1)

    def body(c, carry):
        rows = pl.ds(pl.multiple_of(c * chunk, chunk), chunk)
        cg = blk * nchunk + c
        heads = range(nh)
        gsp = gs_ref[rows, :]
        gb = [_dot(gsp, sel_ref[h]) for h in heads]
        irep = [g[:, :LANES] for g in gb]
        brep = [g[:, LANES:] for g in gb]
        qscale = jnp.asarray(MLSTM_DQK ** -0.5, x_ref.dtype)
        q2 = [x_ref[0, rows, p * LANES:(p + 1) * LANES] * qscale for p in range(nh // 2)]
        k2 = [x_ref[0, rows, k0 + p * LANES:k0 + (p + 1) * LANES] for p in range(nh // 2)]
        zero = jnp.zeros_like(q2[0])
        q = [jnp.where(lo if h % 2 == 0 else ~lo, q2[h // 2], zero) for h in heads]
        k = [jnp.where(lo if h % 2 == 0 else ~lo, k2[h // 2], zero) for h in heads]
        vext = [jnp.concatenate([x_ref[0, rows, v0 + h * dv:v0 + (h + 1) * dv], ones], axis=1)
                for h in heads]
        qk = [_dot_nt(q[h], k2[h // 2]) for h in heads]
        state = [st_ref[h] for h in heads]
        qs = [_dot(q[h], state[h]) for h in heads]
        logd = [jnp.where(incl, brep[h][:, :chunk] - grow_ref[nh + h, pl.ds(cg, 1), :]
                          + grow_ref[h, pl.ds(cg, 1), :], NEG_INF) for h in heads]
        m_intra = [jnp.max(ld, axis=-1, keepdims=True) for ld in logd]
        blast = [b[chunk - 1:chunk, :] for b in brep]
        lw = [blast[h] - brep[h] + irep[h] for h in heads]
        mw = [jnp.max(x, axis=0, keepdims=True) for x in lw]
        m_prev = [m_ref[h:h + 1, :] for h in heads]
        a_inter = [brep[h] + m_prev[h] for h in heads]
        m_t = [jnp.maximum(a_inter[h], m_intra[h]) for h in heads]
        p = [qk[h] * jnp.exp(logd[h] - m_t[h][:, :chunk]) for h in heads]
        pv = [_dot(p[h], vext[h]) for h in heads]
        m_new = [jnp.maximum(blast[h] + m_prev[h], mw[h]) for h in heads]
        for h in heads:
            kw = k[h] * jnp.exp(lw[h] - m_new[h])
            keep = jnp.exp(blast[h] + m_prev[h] - m_new[h])
            st_ref[h] = twice(keep) * state[h] + _dot_tn(kw, vext[h])
            m_ref[h:h + 1, :] = m_new[h]
        for h in heads:
            num = twice(jnp.exp(a_inter[h] - m_t[h])) * qs[h] + pv[h]
            hh = num[:, :dv] / jnp.maximum(jnp.abs(num[:, dv:]), jnp.exp(-m_t[h]))
            mu = jnp.mean(hh, axis=-1, keepdims=True)
            xc = hh - mu
            var = jnp.mean(xc * xc, axis=-1, keepdims=True)
            hn = xc * lax.rsqrt(var + NORM_EPS) * ng_ref[h:h + 1, :]
            og = x_ref[0, rows, og0 + h * dv:og0 + (h + 1) * dv].astype(F32)
            o_ref[0, rows, h * dv:(h + 1) * dv] = (hn * _sigmoid(og)).astype(o_ref.dtype)
        return carry

    lax.fori_loop(0, nchunk, body, 0)


def _mlstm_mixer(proj, gsplit, grow, norm_g, bsz, seq, ts):
    ts = min(ts, seq)
    nblk = seq // ts
    t = bsz * seq
    ncols = proj.shape[1]
    kern = functools.partial(_mlstm_kernel, chunk=CHUNK)
    grow3 = grow.reshape(grow.shape[0], t // CHUNK, CHUNK)
    sel = _gate_selectors(MLSTM_HEADS)
    full = lambda a: pl.BlockSpec(a.shape, lambda b, s: (0,) * a.ndim)
    return pl.pallas_call(
        kern,
        out_shape=jax.ShapeDtypeStruct((bsz, seq, D_MODEL), BF16),
        grid=(bsz, nblk),
        in_specs=[pl.BlockSpec((1, ts, ncols), lambda b, s: (b, s, 0)),
                  pl.BlockSpec((ts, N_SPLIT * LANES), lambda b, s: (b * nblk + s, 0)),
                  full(grow3), full(sel),
                  pl.BlockSpec((MLSTM_HEADS, MLSTM_DV), lambda b, s: (0, 0))],
        out_specs=pl.BlockSpec((1, ts, D_MODEL), lambda b, s: (b, s, 0)),
        scratch_shapes=[pltpu.VMEM((MLSTM_HEADS, LANES, 2 * LANES), F32),
                        pltpu.VMEM((MLSTM_HEADS, LANES), F32)],
        compiler_params=_cparams("parallel", "arbitrary"),
        name="mlstm_mixer")(proj.reshape(bsz, seq, ncols), gsplit, grow3, sel, norm_g)


def _fox_prep_kernel(q_ref, k_ref, g_ref, qo_ref, ko_ref):
    half = _iota2((q_ref.shape[0], LANES), 1) < FOX_DH

    def norm(x_ref, o_ref, g_row, scale):
        for j in range(x_ref.shape[1] // LANES):
            cols = slice(j * LANES, (j + 1) * LANES)
            x = x_ref[:, cols]
            sq = x * x
            s_lo = jnp.sum(jnp.where(half, sq, 0.0), axis=-1, keepdims=True)
            s_all = jnp.sum(sq, axis=-1, keepdims=True)
            ms = jnp.where(half, s_lo, s_all - s_lo) * (1.0 / FOX_DH)
            y = x * lax.rsqrt(ms + NORM_EPS) * g_ref[g_row:g_row + 1, cols]
            o_ref[:, cols] = (y * scale).astype(o_ref.dtype)

    norm(q_ref, qo_ref, 0, FOX_DH ** -0.5 * LOG2E)
    norm(k_ref, ko_ref, 1, 1.0)


def _fox_prep(proj, qk_g, tp):
    t = proj.shape[0]
    tp = min(tp, t)
    g = jnp.tile(qk_g, (1, FOX_HEADS))
    spec = lambda j: pl.BlockSpec((tp, D_MODEL), lambda i: (i, j))
    out = jax.ShapeDtypeStruct((t, D_MODEL), BF16)
    return pl.pallas_call(
        _fox_prep_kernel,
        out_shape=(out, out),
        grid=(t // tp,),
        in_specs=[spec(0), spec(1), pl.BlockSpec((2, D_MODEL), lambda i: (0, 0))],
        out_specs=(spec(0), spec(0)),
        compiler_params=_cparams("parallel"),
        name="fox_prep")(proj, proj, g)


def _fox_kernel(q_ref, k_ref, v_ref, crow_ref, og_ref, theta_ref, o_ref, m_ref, acc_ref, *, tq, wide):
    i = pl.program_id(2)
    hp = pl.program_id(1)
    pair = range(2)
    seq = k_ref.shape[1]
    lo = _iota2((tq, LANES), 1) < FOX_DH
    q = q_ref[0]
    zero = jnp.zeros_like(q)
    qs = (jnp.where(lo, q, zero), jnp.where(lo, zero, q))
    q0 = pl.multiple_of(i * tq, tq)
    cref = [crow_ref[2 * hp + e, :, pl.ds(q0, LANES)][:, 0:1] for e in pair]
    pos = _iota2((1, seq), 1)
    first_live = [jnp.min(jnp.where(cref[e] - crow_ref[2 * hp + e] < -theta_ref[...], seq, pos)) for e in pair]
    n_dead = jnp.minimum(first_live[0], first_live[1])
    m_ref[...] = jnp.full(m_ref.shape, NEG_INF, F32)
    acc_ref[...] = jnp.zeros_like(acc_ref)
    causal = _iota2((tq, tq), 0) >= _iota2((tq, tq), 1)

    def step(k0, width, masked):
        k = k_ref[0, pl.ds(k0, width), :]
        vext = jnp.concatenate([v_ref[0, pl.ds(k0, width), :], jnp.ones((width, LANES), BF16)], axis=1)
        s = [lax.dot_general(qs[e], k, (((1,), (1,)), ((), ())), preferred_element_type=F32) for e in pair]
        for e in pair:
            bias = (cref[e] - crow_ref[2 * hp + e, :, pl.ds(k0, width)]) * LOG2E
            s[e] = s[e] + bias
            if masked:
                s[e] = jnp.where(causal, s[e], NEG_INF)
        m_prev = [m_ref[e] for e in pair]
        m_new = [jnp.maximum(m_prev[e], jnp.max(s[e], axis=-1, keepdims=True)) for e in pair]
        p = [jnp.exp2(s[e] - jnp.concatenate([m_new[e]] * (width // LANES), axis=1)) for e in pair]
        pv = [jnp.dot(p[e].astype(BF16), vext, preferred_element_type=F32) for e in pair]
        for e in pair:
            alpha = jnp.exp2(m_prev[e] - m_new[e])
            acc_ref[e] = jnp.concatenate([alpha, alpha], axis=1) * acc_ref[e] + pv[e]
            m_ref[e] = m_new[e]

    def body(j, carry):
        step(pl.multiple_of(j * (wide * tq), wide * tq), wide * tq, False)
        return carry

    lax.fori_loop(n_dead // (wide * tq), i // wide, body, 0)
    if wide == 2:
        @pl.when(jnp.logical_and(i % 2 == 1, q0 > n_dead))
        def _():
            step(pl.multiple_of((i - 1) * tq, tq), tq, False)
    step(q0, tq, True)
    out = jnp.where(lo, acc_ref[0, :, :LANES] / acc_ref[0, :, LANES:], acc_ref[1, :, :LANES] / acc_ref[1, :, LANES:])
    o_ref[0] = (out * _sigmoid(og_ref[0].astype(F32))).astype(o_ref.dtype)


def _fox_attention(qn, kn, vog, crow, qk_g, bsz, seq, tq, wide):
    tq = min(tq, seq)
    nq = seq // tq
    npair = FOX_HEADS // 2
    kern = functools.partial(_fox_kernel, tq=tq, wide=wide)
    og_blk = D_MODEL // LANES
    vog = vog.reshape(bsz, seq, 2 * D_MODEL)
    qk_bound = 1.02 * FOX_DH ** 0.5 * LOG2E * jnp.max(jnp.abs(qk_g[0])) * jnp.max(jnp.abs(qk_g[1]))
    theta = ((FOX_UNDERFLOW + 2.0 * qk_bound) / LOG2E).reshape(1, 1).astype(F32)
    return pl.pallas_call(
        kern,
        out_shape=jax.ShapeDtypeStruct((bsz, seq, D_MODEL), BF16),
        grid=(bsz, npair, nq),
        in_specs=[pl.BlockSpec((1, tq, LANES), lambda b, h, i: (b, i, h)),
                  pl.BlockSpec((1, seq, LANES), lambda b, h, i: (b, 0, h)),
                  pl.BlockSpec((1, seq, LANES), lambda b, h, i: (b, 0, h)),
                  pl.BlockSpec((FOX_HEADS, 1, seq), lambda b, h, i: (0, 0, b)),
                  pl.BlockSpec((1, tq, LANES), lambda b, h, i: (b, i, og_blk + h)),
                  pl.BlockSpec((1, 1), lambda b, h, i: (0, 0))],
        out_specs=pl.BlockSpec((1, tq, LANES), lambda b, h, i: (b, i, h)),
        scratch_shapes=[pltpu.VMEM((2, tq, LANES), F32), pltpu.VMEM((2, tq, 2 * LANES), F32)],
        compiler_params=_cparams("parallel", "parallel", "arbitrary"),
        name="fox_attention")(qn.reshape(bsz, seq, D_MODEL), kn.reshape(bsz, seq, D_MODEL),
                              vog, crow.reshape(FOX_HEADS, 1, bsz * seq), vog, theta)


def _layer_norm(z, g, b):
    mu = jnp.mean(z, axis=-1, keepdims=True)
    zc = z - mu
    var = jnp.mean(zc * zc, axis=-1, keepdims=True)
    return zc * lax.rsqrt(var + LN_EPS) * g + b


def _outproj_kernel(mix_ref, mem_ref, w_ref, x_ref, g_ref, b_ref, o_ref, ob_ref):
    y = jnp.dot(mix_ref[...], w_ref[0:D_MODEL, :], preferred_element_type=F32)
    y = y + jnp.dot(mem_ref[...], w_ref[D_MODEL:D_MODEL + MEM_W, :], preferred_element_type=F32)
    out = _layer_norm(ALPHA * x_ref[...] + y, g_ref[...], b_ref[...])
    o_ref[...] = out
    ob_ref[...] = out.astype(ob_ref.dtype)


def _outproj_ln(mix, memo, w, x, g, b, tm):
    t = x.shape[0]
    tm = min(tm, t)
    row = lambda n: pl.BlockSpec((tm, n), lambda i: (i, 0))
    const = lambda s: pl.BlockSpec(s, lambda i: (0, 0))
    return pl.pallas_call(
        _outproj_kernel,
        out_shape=(jax.ShapeDtypeStruct((t, D_MODEL), F32), jax.ShapeDtypeStruct((t, D_MODEL), BF16)),
        grid=(t // tm,),
        in_specs=[row(D_MODEL), row(MEM_W), const((D_MODEL + MEM_W, D_MODEL)), row(D_MODEL),
                  const((1, D_MODEL)), const((1, D_MODEL))],
        out_specs=(row(D_MODEL), row(D_MODEL)),
        compiler_params=_cparams("parallel"),
        name="outproj_ln")(mix, memo, w, x, g[None, :], b[None, :])


def _ffn_kernel(xb_ref, x_ref, wu_ref, wd_ref, g_ref, b_ref, o_ref, ob_ref, *, fchunk):
    xb = xb_ref[...]
    y = jnp.zeros(x_ref.shape, F32)
    for c in range(D_FF // fchunk):
        h = jnp.dot(xb, wu_ref[:, c * fchunk:(c + 1) * fchunk], preferred_element_type=F32)
        h = jnp.square(jnp.maximum(h, 0.0)).astype(BF16)
        y = y + jnp.dot(h, wd_ref[c * fchunk:(c + 1) * fchunk, :], preferred_element_type=F32)
    out = _layer_norm(ALPHA * x_ref[...] + y, g_ref[...], b_ref[...])
    o_ref[...] = out
    ob_ref[...] = out.astype(ob_ref.dtype)


def _ffn_ln(xb, x, wu, wd, g, b, tm, fchunk):
    t = x.shape[0]
    tm = min(tm, t)
    row = pl.BlockSpec((tm, D_MODEL), lambda i: (i, 0))
    const = lambda s: pl.BlockSpec(s, lambda i: (0, 0), pipeline_mode=pl.Buffered(1))
    kern = functools.partial(_ffn_kernel, fchunk=fchunk)
    return pl.pallas_call(
        kern,
        out_shape=(jax.ShapeDtypeStruct((t, D_MODEL), F32), jax.ShapeDtypeStruct((t, D_MODEL), BF16)),
        grid=(t // tm,),
        in_specs=[row, row, const((D_MODEL, D_FF)), const((D_FF, D_MODEL)),
                  const((1, D_MODEL)), const((1, D_MODEL))],
        out_specs=(row, row),
        compiler_params=_cparams("parallel"),
        name="ffn_ln")(xb, x, wu, wd, g[None, :], b[None, :])


def kernel(x, mem, gdn_w_in, gdn_conv_w, gdn_a_log, gdn_dt_bias, gdn_norm_g, mlstm_w_in, mlstm_b_gate,
           mlstm_norm_g, fox_w_in, fox_b_f, fox_qk_g, mem_w_kv, w_out, ln1_g, ln1_b, w_up, w_down,
           ln2_g, ln2_b):
    bsz, seq, _ = x.shape
    t = bsz * seq
    n_mem = mem.shape[1]
    x = x.reshape(t, D_MODEL)
    xb = x.astype(BF16)
    memb = mem.reshape(bsz * n_mem, D_MODEL).astype(BF16)
    for i in range(DEPTH):
        kind, j = i % N_MIXERS, i // N_MIXERS
        if kind == 0:
            w_in, n_main, n_gate = gdn_w_in[j], 4 * D_MODEL, 2 * GDN_HEADS
        elif kind == 1:
            w_in, n_main, n_gate = mlstm_w_in[j], 3 * D_MODEL, 2 * MLSTM_HEADS
        else:
            w_in, n_main, n_gate = fox_w_in[j], 4 * D_MODEL, FOX_HEADS
        w_gate = w_in[:, n_main:n_main + n_gate]
        w_memq = w_in[:, n_main + n_gate:].astype(BF16)
        proj_in = lambda lo, hi, dt: _matmul(xb, w_in[:, lo:hi].astype(BF16), dt, 1024, 1024, "in_proj")
        kv = _matmul(memb, mem_w_kv[i].astype(BF16), BF16, 512, 1024, "mem_kv")
        memo = _mem_attention(xb, w_memq, kv.reshape(bsz, n_mem, 2 * MEM_W), bsz, seq, 512)
        if kind == 0:
            grow, gsplit = _gates(xb, w_gate, gdn_a_log[j], gdn_dt_bias[j], "gdn", bsz, seq, 512)
            mix = _gdn_mixer(proj_in(0, GDN_QKV, F32), proj_in(GDN_QKV, n_main, BF16), gsplit, grow,
                             gdn_conv_w[j], gdn_norm_g[j], bsz, seq, 512, 4)
        elif kind == 1:
            bias = jnp.concatenate([mlstm_b_gate[j, 0], mlstm_b_gate[j, 1]])
            grow, gsplit = _gates(xb, w_gate, bias, bias, "mlstm", bsz, seq, 512)
            mix = _mlstm_mixer(proj_in(0, n_main, BF16), gsplit, grow, mlstm_norm_g[j], bsz, seq, 512)
        else:
            crow = _gates(xb, w_gate, fox_b_f[j], fox_b_f[j], "fox", bsz, seq, 512)
            qn, kn = _fox_prep(proj_in(0, 2 * D_MODEL, F32), fox_qk_g[j], 512)
            mix = _fox_attention(qn, kn, proj_in(2 * D_MODEL, n_main, BF16), crow, fox_qk_g[j], bsz, seq, 512, 2)
        x, xb = _outproj_ln(mix.reshape(t, D_MODEL), memo, w_out[i].astype(BF16), x, ln1_g[i], ln1_b[i], 512)
        x, xb = _ffn_ln(xb, x, w_up[i].astype(BF16), w_down[i].astype(BF16), ln2_g[i], ln2_b[i], 512, 1024)
    return x.reshape(bsz, seq, D_MODEL)
```

```python
import functools
import math

import jax
import jax.numpy as jnp
from jax import lax
from jax.experimental import pallas as pl
from jax.experimental.pallas import tpu as pltpu

F32 = jnp.float32
BF16 = jnp.bfloat16

D_MODEL = 1024
DEPTH = 4
N_MIXERS = 3
GDN_HEADS = 8
GDN_D = 128
GDN_QKV = 3 * GDN_HEADS * GDN_D
CONV_K = 4
MLSTM_HEADS = 8
MLSTM_DQK = 64
MLSTM_DV = 128
FOX_HEADS = 16
FOX_DH = 64
MEM_HEADS = 4
MEM_DH = 128
MEM_W = MEM_HEADS * MEM_DH
D_FF = 4 * D_MODEL
ALPHA = (2 * DEPTH) ** 0.25
LN_EPS = 1e-5
NORM_EPS = 1e-6

LANES = 128
SUBLANES = 8
VMEM_LIMIT = 56 * 1024 * 1024

CHUNK = 64
N_GATE_ROWS = 16
N_SPLIT = 3
INV_BLOCK = 16
NEG_INF = float("-inf")
LOG2E = math.log2(math.e)
FOX_UNDERFLOW = 160.0


def _cparams(*sem):
    return pltpu.CompilerParams(dimension_semantics=sem, vmem_limit_bytes=VMEM_LIMIT)


def _dot(a, b):
    return jnp.dot(a.astype(BF16), b.astype(BF16), preferred_element_type=F32)


def _dot_nt(a, b):
    return lax.dot_general(a.astype(BF16), b.astype(BF16), (((1,), (1,)), ((), ())),
                           preferred_element_type=F32)


def _dot_tn(a, b):
    return lax.dot_general(a.astype(BF16), b.astype(BF16), (((0,), (0,)), ((), ())),
                           preferred_element_type=F32)


def _sigmoid(x):
    return 1.0 / (1.0 + jnp.exp(-x))


def _softplus(x):
    return jnp.maximum(x, 0.0) + jnp.log1p(jnp.exp(-jnp.abs(x)))


def _iota2(shape, dim):
    return lax.broadcasted_iota(jnp.int32, shape, dim)


def _mm_kernel(x_ref, w_ref, o_ref):
    o_ref[...] = jnp.dot(x_ref[...], w_ref[...], preferred_element_type=F32).astype(o_ref.dtype)


def _matmul(x, w, out_dtype, tm, tn, name):
    m, k = x.shape
    n = w.shape[1]
    tm, tn = min(tm, m), min(tn, n)
    assert m % tm == 0 and n % tn == 0
    return pl.pallas_call(
        _mm_kernel,
        out_shape=jax.ShapeDtypeStruct((m, n), out_dtype),
        grid=(m // tm, n // tn),
        in_specs=[pl.BlockSpec((tm, k), lambda i, j: (i, 0)),
                  pl.BlockSpec((k, tn), lambda i, j: (0, j))],
        out_specs=pl.BlockSpec((tm, tn), lambda i, j: (i, j)),
        compiler_params=_cparams("parallel", "parallel"),
        name=name)(x, w)


def _scan_rows(x, seg):
    pos = _iota2(x.shape, 0) & (seg - 1)
    d = 1
    while d < seg:
        x = x + jnp.where(pos >= d, pltpu.roll(x, d, 0), 0.0)
        d *= 2
    return x


def _scan_lanes(x, seg):
    pos = _iota2(x.shape, 1) & (seg - 1)
    d = 1
    while d < seg:
        x = x + jnp.where(pos >= d, pltpu.roll(x, d, 1), 0.0)
        d *= 2
    return x


def _gate_act(kind, y, p1, p2, idx):
    if kind == "gdn":
        g = -jnp.exp(p1) * _softplus(y + p2)
        val = jnp.where(idx < GDN_HEADS, g, _sigmoid(y))
        scan = idx < GDN_HEADS
    elif kind == "mlstm":
        z = y + p2
        val = jnp.where(idx < MLSTM_HEADS, z, -_softplus(-z))
        scan = idx >= MLSTM_HEADS
    else:
        val = -_softplus(-(y + p2))
        scan = idx >= 0
    return val, scan


def _gates_kernel(x_ref, wg_ref, wgt_ref, p1_ref, p2_ref, p1t_ref, p2t_ref, row_ref, *rest, kind, seg):
    full = kind == "fox"
    tg = x_ref.shape[0]
    x = x_ref[...]
    yrow = lax.dot_general(wgt_ref[...], x, (((1,), (1,)), ((), ())),
                           preferred_element_type=F32)
    vrow, srow = _gate_act(kind, yrow, p1t_ref[...], p2t_ref[...], _iota2(yrow.shape, 0))
    crow = _scan_lanes(vrow, tg if full else seg)
    if full:
        (carry_ref,) = rest

        @pl.when(pl.program_id(1) == 0)
        def _():
            carry_ref[...] = jnp.zeros_like(carry_ref)
        crow = crow + carry_ref[...]
        carry_ref[...] = crow[:, tg - 1:tg]
        row_ref[...] = crow[:N_GATE_ROWS, :]
        return
    split_ref, _ = rest
    row_ref[...] = jnp.where(srow, crow, vrow)[:N_GATE_ROWS, :]
    ycol = jnp.dot(x, wg_ref[...], preferred_element_type=F32)
    vcol, scol = _gate_act(kind, ycol, p1_ref[...], p2_ref[...], _iota2(ycol.shape, 1))
    col = jnp.where(scol, _scan_rows(vcol, seg), vcol)
    hi = col.astype(BF16)
    r1 = col - hi.astype(F32)
    mid = r1.astype(BF16)
    lo = (r1 - mid.astype(F32)).astype(BF16)
    split_ref[:, 0:LANES] = hi
    split_ref[:, LANES:2 * LANES] = mid
    split_ref[:, 2 * LANES:3 * LANES] = lo


def _gates(xb, wg, p1, p2, kind, bsz, seq, tg):
    ng = wg.shape[1]
    wgp = jnp.pad(wg, ((0, 0), (0, LANES - ng))).astype(BF16)
    p1p = jnp.pad(p1, (0, LANES - p1.shape[0])).astype(F32)
    p2p = jnp.pad(p2, (0, LANES - p2.shape[0])).astype(F32)
    tg = min(tg, seq)
    nblk = seq // tg
    t = bsz * seq
    kern = functools.partial(_gates_kernel, kind=kind, seg=CHUNK)
    const = lambda b, s: (0, 0)
    row_shape = jax.ShapeDtypeStruct((N_GATE_ROWS, t), F32)
    row_spec = pl.BlockSpec((N_GATE_ROWS, tg), lambda b, s: (0, b * nblk + s))
    if kind == "fox":
        out_shape, out_specs = row_shape, row_spec
    else:
        out_shape = (row_shape, jax.ShapeDtypeStruct((t, N_SPLIT * LANES), BF16))
        out_specs = (row_spec, pl.BlockSpec((tg, N_SPLIT * LANES), lambda b, s: (b * nblk + s, 0)))
    return pl.pallas_call(
        kern,
        out_shape=out_shape,
        grid=(bsz, nblk),
        in_specs=[pl.BlockSpec((tg, D_MODEL), lambda b, s: (b * nblk + s, 0)),
                  pl.BlockSpec((D_MODEL, LANES), const),
                  pl.BlockSpec((LANES, D_MODEL), const),
                  pl.BlockSpec((1, LANES), const), pl.BlockSpec((1, LANES), const),
                  pl.BlockSpec((LANES, 1), const), pl.BlockSpec((LANES, 1), const)],
        out_specs=out_specs,
        scratch_shapes=[pltpu.VMEM((LANES, 1), F32)],
        compiler_params=_cparams("parallel", "arbitrary"),
        name=f"gates_{kind}")(xb, wgp, wgp.T, p1p[None, :], p2p[None, :], p1p[:, None], p2p[:, None])


def _gate_selectors(n_heads):
    r = jnp.arange(N_SPLIT * LANES)[None, :, None] % LANES
    c = jnp.arange(2 * LANES)[None, None, :]
    h = jnp.arange(n_heads)[:, None, None]
    return jnp.where(c < LANES, r == h, r == n_heads + h).astype(BF16)


def _memattn_kernel(x_ref, wq_ref, kv_ref, o_ref):
    q = jnp.dot(x_ref[...], wq_ref[...], preferred_element_type=F32).astype(BF16)
    scale = MEM_DH ** -0.5
    heads = range(MEM_HEADS)
    hc = [slice(h * MEM_DH, (h + 1) * MEM_DH) for h in heads]
    ones = jnp.ones((kv_ref.shape[1], LANES), BF16)
    logits = [_dot_nt(q[:, hc[h]], kv_ref[0, :, hc[h]]) * scale for h in heads]
    e = [jnp.exp(lg - jnp.max(lg, axis=-1, keepdims=True)) for lg in logits]
    pv = [_dot(e[h], jnp.concatenate([kv_ref[0, :, MEM_W + h * MEM_DH:MEM_W + (h + 1) * MEM_DH], ones], axis=1))
          for h in heads]
    for h in heads:
        o_ref[:, hc[h]] = (pv[h][:, :MEM_DH] / pv[h][:, MEM_DH:]).astype(o_ref.dtype)


def _mem_attention(xb, wq, kv, bsz, seq, tm):
    tm = min(tm, seq)
    nblk = seq // tm
    n_mem = kv.shape[1]
    return pl.pallas_call(
        _memattn_kernel,
        out_shape=jax.ShapeDtypeStruct((bsz * seq, MEM_W), BF16),
        grid=(bsz, nblk),
        in_specs=[pl.BlockSpec((tm, D_MODEL), lambda b, s: (b * nblk + s, 0)),
                  pl.BlockSpec((D_MODEL, MEM_W), lambda b, s: (0, 0)),
                  pl.BlockSpec((1, n_mem, 2 * MEM_W), lambda b, s: (b, 0, 0))],
        out_specs=pl.BlockSpec((tm, MEM_W), lambda b, s: (b * nblk + s, 0)),
        compiler_params=_cparams("parallel", "parallel"),
        name="mem_attention")(xb, wq, kv)


def _unit_lower_inverse(a_list):
    c = a_list[0].shape[0]
    ri, ci = _iota2((c, c), 0), _iota2((c, c), 1)
    eye = (ri == ci).astype(F32)
    shift = int(math.log2(INV_BLOCK))
    same_block = (ri >> shift) == (ci >> shift)
    d = [jnp.where(same_block, a, 0.0) for a in a_list]
    n = [a - di for a, di in zip(a_list, d)]
    t = [eye - di for di in d]
    p = d
    for _ in range(shift - 1):
        p = [_dot(pi, pi) for pi in p]
        t = [ti + _dot(ti, pi) for ti, pi in zip(t, p)]
    nb = c // INV_BLOCK
    if nb > 1:
        m = [_dot(ti, ni) for ti, ni in zip(t, n)]
        x = [eye - mi for mi in m]
        p = m
        for _ in range(int(math.log2(nb)) - 1):
            p = [_dot(pi, pi) for pi in p]
            x = [xi + _dot(xi, pi) for xi, pi in zip(x, p)]
        t = [_dot(xi, ti) for xi, ti in zip(x, t)]
    return t


def _gdn_kernel(x_ref, z_ref, gs_ref, grow_ref, sel_ref, cw_ref, ng_ref, o_ref,
                ext_ref, qkv_ref, u_ref, w_ref, qg_ref, kg_ref, attn_ref, dl_ref, st_ref, *, chunk, group):
    ts = x_ref.shape[1]
    nchunk = ts // chunk
    ngroup = nchunk // group
    grows = group * chunk
    s_idx = pl.program_id(1)
    nblk = pl.num_programs(1) - 1
    blk_prev = pl.program_id(0) * nblk + s_idx - 1
    cur = s_idx % 2
    prev = 1 - cur
    halo = SUBLANES
    nqk = GDN_HEADS * GDN_D

    def conv_rows(r0, nrows):
        for j in range(GDN_QKV // LANES):
            cols = slice(j * LANES, (j + 1) * LANES)
            acc = cw_ref[CONV_K - 1:CONV_K, cols] * ext_ref[halo + r0:halo + r0 + nrows, cols]
            for tap in range(CONV_K - 1):
                off = halo - (CONV_K - 1) + tap + r0
                acc = acc + cw_ref[tap:tap + 1, cols] * ext_ref[off:off + nrows, cols]
            y = acc * _sigmoid(acc)
            if j < 2 * GDN_HEADS:
                y = y * lax.rsqrt(jnp.sum(y * y, axis=-1, keepdims=True) + NORM_EPS)
            if j < GDN_HEADS:
                y = y * (GDN_D ** -0.5)
            qkv_ref[cur, r0:r0 + nrows, cols] = y

    ri, ci = _iota2((chunk, chunk), 0), _iota2((chunk, chunk), 1)
    incl = ri >= ci
    strict = ri > ci

    def intra(gi):
        items = [(cc, h) for cc in range(group) for h in range(GDN_HEADS)]
        cidx = [gi * group + cc for cc in range(group)]
        rows = [slice(c * chunk, (c + 1) * chunk) for c in cidx]
        gsp = [gs_ref[r, :] for r in rows]
        gb = [_dot(gsp[cc], sel_ref[h]) for cc, h in items]
        gcc = [g[:, :LANES] for g in gb]
        bc = [g[:, LANES:] for g in gb]
        q = [qkv_ref[prev, rows[cc], h * GDN_D:(h + 1) * GDN_D] for cc, h in items]
        k = [qkv_ref[prev, rows[cc], nqk + h * GDN_D:nqk + (h + 1) * GDN_D] for cc, h in items]
        v = [qkv_ref[prev, rows[cc], 2 * nqk + h * GDN_D:2 * nqk + (h + 1) * GDN_D] for cc, h in items]
        decay = [jnp.exp(jnp.where(incl, gcc[n][:, :chunk]
                                   - grow_ref[h, pl.ds(blk_prev * nchunk + cidx[cc], 1), :], NEG_INF))
                 for n, (cc, h) in enumerate(items)]
        kb = [ki * bi for ki, bi in zip(k, bc)]
        kq = [_dot_nt(jnp.concatenate([kbi, qi], axis=0), ki) for kbi, qi, ki in zip(kb, q, k)]
        a = [jnp.where(strict, kqi[:chunk] * di, 0.0) for kqi, di in zip(kq, decay)]
        tinv = _unit_lower_inverse(a)
        egc = [jnp.exp(g) for g in gcc]
        sol = [_dot(ti, jnp.concatenate([vi * bi, kbi * ei], axis=1))
               for ti, vi, bi, kbi, ei in zip(tinv, v, bc, kb, egc)]
        for n, (cc, h) in enumerate(items):
            hc = slice(h * GDN_D, (h + 1) * GDN_D)
            glast = gcc[n][chunk - 1:chunk, :]
            u_ref[rows[cc], hc] = sol[n][:, :GDN_D]
            w_ref[rows[cc], hc] = sol[n][:, GDN_D:].astype(BF16)
            qg_ref[rows[cc], hc] = (q[n] * egc[n]).astype(BF16)
            kg_ref[rows[cc], hc] = (k[n] * jnp.exp(glast - gcc[n])).astype(BF16)
            attn_ref[h, rows[cc], :] = (kq[n][chunk:] * decay[n]).astype(BF16)
            dl_ref[h, cidx[cc]:cidx[cc] + 1, :] = jnp.exp(glast)

    def inter(c, carry):
        rows = pl.ds(pl.multiple_of(c * chunk, chunk), chunk)
        heads = range(GDN_HEADS)
        hcs = [slice(h * GDN_D, (h + 1) * GDN_D) for h in heads]
        ng = ng_ref[...]
        state = [st_ref[h] for h in heads]
        wq = [_dot(jnp.concatenate([w_ref[rows, hcs[h]], qg_ref[rows, hcs[h]]], axis=0), state[h])
              for h in heads]
        v_new = [(u_ref[rows, hcs[h]] - wq[h][:chunk]).astype(BF16) for h in heads]
        o2 = [_dot(attn_ref[h, rows, :], v_new[h]) for h in heads]
        upd = [_dot_tn(kg_ref[rows, hcs[h]], v_new[h]) for h in heads]
        for h in heads:
            st_ref[h] = state[h] * dl_ref[h, pl.ds(c, 1), :] + upd[h]
        for h in heads:
            o = wq[h][chunk:] + o2[h]
            z = z_ref[0, rows, hcs[h]].astype(F32)
            on = o * lax.rsqrt(jnp.mean(o * o, axis=-1, keepdims=True) + NORM_EPS) * ng
            o_ref[0, rows, hcs[h]] = (on * (z * _sigmoid(z))).astype(o_ref.dtype)
        return carry

    @pl.when(s_idx == 0)
    def _():
        st_ref[...] = jnp.zeros_like(st_ref)
        o_ref[...] = jnp.zeros_like(o_ref)
        ext_ref[0:halo, :] = jnp.zeros((halo, GDN_QKV), F32)
        ext_ref[halo:halo + ts, :] = x_ref[0]
        conv_rows(0, ts)
        ext_ref[0:halo, :] = ext_ref[ts:ts + halo, :]

    @pl.when(jnp.logical_and(s_idx > 0, s_idx < nblk))
    def _():
        ext_ref[halo:halo + ts, :] = x_ref[0]
        for gi in range(ngroup):
            intra(gi)
            conv_rows(gi * grows, grows)
        ext_ref[0:halo, :] = ext_ref[ts:ts + halo, :]
        lax.fori_loop(0, nchunk, inter, 0)

    @pl.when(s_idx == nblk)
    def _():
        for gi in range(ngroup):
            intra(gi)
        lax.fori_loop(0, nchunk, inter, 0)


def _gdn_mixer(proj, zproj, gsplit, grow, conv_w, norm_g, bsz, seq, ts, group):
    ts = min(ts, seq)
    nblk = seq // ts
    t = bsz * seq
    ncols = proj.shape[1]
    assert ncols == GDN_QKV
    nchunk = ts // CHUNK
    kern = functools.partial(_gdn_kernel, chunk=CHUNK, group=min(group, nchunk))
    grow3 = grow.reshape(grow.shape[0], t // CHUNK, CHUNK)
    sel = _gate_selectors(GDN_HEADS)
    full = lambda a: pl.BlockSpec(a.shape, lambda b, s: (0,) * a.ndim, pipeline_mode=pl.Buffered(1))
    nxt = lambda s: jnp.minimum(s, nblk - 1)
    prv = lambda s: jnp.maximum(s - 1, 0)
    return pl.pallas_call(
        kern,
        out_shape=jax.ShapeDtypeStruct((bsz, seq, D_MODEL), BF16),
        grid=(bsz, nblk + 1),
        in_specs=[pl.BlockSpec((1, ts, ncols), lambda b, s: (b, nxt(s), 0)),
                  pl.BlockSpec((1, ts, D_MODEL), lambda b, s: (b, prv(s), 0)),
                  pl.BlockSpec((ts, N_SPLIT * LANES), lambda b, s: (b * nblk + prv(s), 0)),
                  full(grow3), full(sel),
                  pl.BlockSpec((CONV_K, GDN_QKV), lambda b, s: (0, 0)),
                  pl.BlockSpec((1, GDN_D), lambda b, s: (0, 0))],
        out_specs=pl.BlockSpec((1, ts, D_MODEL), lambda b, s: (b, prv(s), 0)),
        scratch_shapes=[pltpu.VMEM((ts + 2 * SUBLANES, GDN_QKV), F32),
                        pltpu.VMEM((2, ts, GDN_QKV), F32),
                        pltpu.VMEM((ts, D_MODEL), F32),
                        pltpu.VMEM((ts, D_MODEL), BF16),
                        pltpu.VMEM((ts, D_MODEL), BF16),
                        pltpu.VMEM((ts, D_MODEL), BF16),
                        pltpu.VMEM((GDN_HEADS, ts, CHUNK), BF16),
                        pltpu.VMEM((GDN_HEADS, nchunk, LANES), F32),
                        pltpu.VMEM((GDN_HEADS, GDN_D, GDN_D), F32)],
        compiler_params=_cparams("parallel", "arbitrary"),
        name="gdn_mixer")(proj.reshape(bsz, seq, ncols), zproj.reshape(bsz, seq, D_MODEL), gsplit, grow3,
                          sel, conv_w, norm_g[None, :])


def _mlstm_kernel(x_ref, gs_ref, grow_ref, sel_ref, ng_ref, o_ref, st_ref, m_ref, *, chunk):
    ts = x_ref.shape[1]
    nchunk = ts // chunk
    s_idx = pl.program_id(1)
    blk = pl.program_id(0) * pl.num_programs(1) + s_idx
    nh, dv = MLSTM_HEADS, MLSTM_DV
    k0, v0, og0 = nh * MLSTM_DQK, 2 * nh * MLSTM_DQK, 2 * nh * MLSTM_DQK + nh * dv

    @pl.when(s_idx == 0)
    def _():
        st_ref[...] = jnp.zeros_like(st_ref)
        m_ref[...] = jnp.zeros_like(m_ref)

    incl = _iota2((chunk, chunk), 0) >= _iota2((chunk, chunk), 1)
    lo = _iota2((chunk, LANES), 1) < MLSTM_DQK
    ones = jnp.ones((chunk, LANES), x_ref.dtype)
    twice = lambda r: jnp.concatenate([r, r], axis=1)

    def body(c, carry):
        rows = pl.ds(pl.multiple_of(c * chunk, chunk), chunk)
        cg = blk * nchunk + c
        heads = range(nh)
        gsp = gs_ref[rows, :]
        gb = [_dot(gsp, sel_ref[h]) for h in heads]
        irep = [g[:, :LANES] for g in gb]
        brep = [g[:, LANES:] for g in gb]
        qscale = jnp.asarray(MLSTM_DQK ** -0.5, x_ref.dtype)
        q2 = [x_ref[0, rows, p * LANES:(p + 1) * LANES] * qscale for p in range(nh // 2)]
        k2 = [x_ref[0, rows, k0 + p * LANES:k0 + (p + 1) * LANES] for p in range(nh // 2)]
        zero = jnp.zeros_like(q2[0])
        q = [jnp.where(lo if h % 2 == 0 else ~lo, q2[h // 2], zero) for h in heads]
        k = [jnp.where(lo if h % 2 == 0 else ~lo, k2[h // 2], zero) for h in heads]
        vext = [jnp.concatenate([x_ref[0, rows, v0 + h * dv:v0 + (h + 1) * dv], ones], axis=1)
                for h in heads]
        qk = [_dot_nt(q[h], k2[h // 2]) for h in heads]
        state = [st_ref[h] for h in heads]
        qs = [_dot(q[h], state[h]) for h in heads]
        logd = [jnp.where(incl, brep[h][:, :chunk] - grow_ref[nh + h, pl.ds(cg, 1), :]
                          + grow_ref[h, pl.ds(cg, 1), :], NEG_INF) for h in heads]
        m_intra = [jnp.max(ld, axis=-1, keepdims=True) for ld in logd]
        blast = [b[chunk - 1:chunk, :] for b in brep]
        lw = [blast[h] - brep[h] + irep[h] for h in heads]
        mw = [jnp.max(x, axis=0, keepdims=True) for x in lw]
        m_prev = [m_ref[h:h + 1, :] for h in heads]
        a_inter = [brep[h] + m_prev[h] for h in heads]
        m_t = [jnp.maximum(a_inter[h], m_intra[h]) for h in heads]
        p = [qk[h] * jnp.exp(logd[h] - m_t[h][:, :chunk]) for h in heads]
        pv = [_dot(p[h], vext[h]) for h in heads]
        m_new = [jnp.maximum(blast[h] + m_prev[h], mw[h]) for h in heads]
        for h in heads:
            kw = k[h] * jnp.exp(lw[h] - m_new[h])
            keep = jnp.exp(blast[h] + m_prev[h] - m_new[h])
            st_ref[h] = twice(keep) * state[h] + _dot_tn(kw, vext[h])
            m_ref[h:h + 1, :] = m_new[h]
        for h in heads:
            num = twice(jnp.exp(a_inter[h] - m_t[h])) * qs[h] + pv[h]
            hh = num[:, :dv] / jnp.maximum(jnp.abs(num[:, dv:]), jnp.exp(-m_t[h]))
            mu = jnp.mean(hh, axis=-1, keepdims=True)
            xc = hh - mu
            var = jnp.mean(xc * xc, axis=-1, keepdims=True)
            hn = xc * lax.rsqrt(var + NORM_EPS) * ng_ref[h:h + 1, :]
            og = x_ref[0, rows, og0 + h * dv:og0 + (h + 1) * dv].astype(F32)
            o_ref[0, rows, h * dv:(h + 1) * dv] = (hn * _sigmoid(og)).astype(o_ref.dtype)
        return carry

    lax.fori_loop(0, nchunk, body, 0)


def _mlstm_mixer(proj, gsplit, grow, norm_g, bsz, seq, ts):
    ts = min(ts, seq)
    nblk = seq // ts
    t = bsz * seq
    ncols = proj.shape[1]
    kern = functools.partial(_mlstm_kernel, chunk=CHUNK)
    grow3 = grow.reshape(grow.shape[0], t // CHUNK, CHUNK)
    sel = _gate_selectors(MLSTM_HEADS)
    full = lambda a: pl.BlockSpec(a.shape, lambda b, s: (0,) * a.ndim)
    return pl.pallas_call(
        kern,
        out_shape=jax.ShapeDtypeStruct((bsz, seq, D_MODEL), BF16),
        grid=(bsz, nblk),
        in_specs=[pl.BlockSpec((1, ts, ncols), lambda b, s: (b, s, 0)),
                  pl.BlockSpec((ts, N_SPLIT * LANES), lambda b, s: (b * nblk + s, 0)),
                  full(grow3), full(sel),
                  pl.BlockSpec((MLSTM_HEADS, MLSTM_DV), lambda b, s: (0, 0))],
        out_specs=pl.BlockSpec((1, ts, D_MODEL), lambda b, s: (b, s, 0)),
        scratch_shapes=[pltpu.VMEM((MLSTM_HEADS, LANES, 2 * LANES), F32),
                        pltpu.VMEM((MLSTM_HEADS, LANES), F32)],
        compiler_params=_cparams("parallel", "arbitrary"),
        name="mlstm_mixer")(proj.reshape(bsz, seq, ncols), gsplit, grow3, sel, norm_g)


def _fox_prep_kernel(q_ref, k_ref, g_ref, qo_ref, ko_ref):
    half = _iota2((q_ref.shape[0], LANES), 1) < FOX_DH

    def norm(x_ref, o_ref, g_row, scale):
        for j in range(x_ref.shape[1] // LANES):
            cols = slice(j * LANES, (j + 1) * LANES)
            x = x_ref[:, cols]
            sq = x * x
            s_lo = jnp.sum(jnp.where(half, sq, 0.0), axis=-1, keepdims=True)
            s_all = jnp.sum(sq, axis=-1, keepdims=True)
            ms = jnp.where(half, s_lo, s_all - s_lo) * (1.0 / FOX_DH)
            y = x * lax.rsqrt(ms + NORM_EPS) * g_ref[g_row:g_row + 1, cols]
            o_ref[:, cols] = (y * scale).astype(o_ref.dtype)

    norm(q_ref, qo_ref, 0, FOX_DH ** -0.5 * LOG2E)
    norm(k_ref, ko_ref, 1, 1.0)


def _fox_prep(proj, qk_g, tp):
    t = proj.shape[0]
    tp = min(tp, t)
    g = jnp.tile(qk_g, (1, FOX_HEADS))
    spec = lambda j: pl.BlockSpec((tp, D_MODEL), lambda i: (i, j))
    out = jax.ShapeDtypeStruct((t, D_MODEL), BF16)
    return pl.pallas_call(
        _fox_prep_kernel,
        out_shape=(out, out),
        grid=(t // tp,),
        in_specs=[spec(0), spec(1), pl.BlockSpec((2, D_MODEL), lambda i: (0, 0))],
        out_specs=(spec(0), spec(0)),
        compiler_params=_cparams("parallel"),
        name="fox_prep")(proj, proj, g)


def _fox_kernel(q_ref, k_ref, v_ref, crow_ref, og_ref, theta_ref, o_ref, m_ref, acc_ref, *, tq, wide):
    i = pl.program_id(2)
    hp = pl.program_id(1)
    pair = range(2)
    seq = k_ref.shape[1]
    lo = _iota2((tq, LANES), 1) < FOX_DH
    q = q_ref[0]
    zero = jnp.zeros_like(q)
    qs = (jnp.where(lo, q, zero), jnp.where(lo, zero, q))
    q0 = pl.multiple_of(i * tq, tq)
    cref = [crow_ref[2 * hp + e, :, pl.ds(q0, LANES)][:, 0:1] for e in pair]
    pos = _iota2((1, seq), 1)
    first_live = [jnp.min(jnp.where(cref[e] - crow_ref[2 * hp + e] < -theta_ref[...], seq, pos)) for e in pair]
    n_dead = jnp.minimum(first_live[0], first_live[1])
    m_ref[...] = jnp.full(m_ref.shape, NEG_INF, F32)
    acc_ref[...] = jnp.zeros_like(acc_ref)
    causal = _iota2((tq, tq), 0) >= _iota2((tq, tq), 1)

    def step(k0, width, masked):
        k = k_ref[0, pl.ds(k0, width), :]
        vext = jnp.concatenate([v_ref[0, pl.ds(k0, width), :], jnp.ones((width, LANES), BF16)], axis=1)
        s = [lax.dot_general(qs[e], k, (((1,), (1,)), ((), ())), preferred_element_type=F32) for e in pair]
        for e in pair:
            bias = (cref[e] - crow_ref[2 * hp + e, :, pl.ds(k0, width)]) * LOG2E
            s[e] = s[e] + bias
            if masked:
                s[e] = jnp.where(causal, s[e], NEG_INF)
        m_prev = [m_ref[e] for e in pair]
        m_new = [jnp.maximum(m_prev[e], jnp.max(s[e], axis=-1, keepdims=True)) for e in pair]
        p = [jnp.exp2(s[e] - jnp.concatenate([m_new[e]] * (width // LANES), axis=1)) for e in pair]
        pv = [jnp.dot(p[e].astype(BF16), vext, preferred_element_type=F32) for e in pair]
        for e in pair:
            alpha = jnp.exp2(m_prev[e] - m_new[e])
            acc_ref[e] = jnp.concatenate([alpha, alpha], axis=1) * acc_ref[e] + pv[e]
            m_ref[e] = m_new[e]

    def body(j, carry):
        step(pl.multiple_of(j * (wide * tq), wide * tq), wide * tq, False)
        return carry

    lax.fori_loop(n_dead // (wide * tq), i // wide, body, 0)
    if wide == 2:
        @pl.when(jnp.logical_and(i % 2 == 1, q0 > n_dead))
        def _():
            step(pl.multiple_of((i - 1) * tq, tq), tq, False)
    step(q0, tq, True)
    out = jnp.where(lo, acc_ref[0, :, :LANES] / acc_ref[0, :, LANES:], acc_ref[1, :, :LANES] / acc_ref[1, :, LANES:])
    o_ref[0] = (out * _sigmoid(og_ref[0].astype(F32))).astype(o_ref.dtype)


def _fox_attention(qn, kn, vog, crow, qk_g, bsz, seq, tq, wide):
    tq = min(tq, seq)
    nq = seq // tq
    npair = FOX_HEADS // 2
    kern = functools.partial(_fox_kernel, tq=tq, wide=wide)
    og_blk = D_MODEL // LANES
    vog = vog.reshape(bsz, seq, 2 * D_MODEL)
    qk_bound = 1.02 * FOX_DH ** 0.5 * LOG2E * jnp.max(jnp.abs(qk_g[0])) * jnp.max(jnp.abs(qk_g[1]))
    theta = ((FOX_UNDERFLOW + 2.0 * qk_bound) / LOG2E).reshape(1, 1).astype(F32)
    return pl.pallas_call(
        kern,
        out_shape=jax.ShapeDtypeStruct((bsz, seq, D_MODEL), BF16),
        grid=(bsz, npair, nq),
        in_specs=[pl.BlockSpec((1, tq, LANES), lambda b, h, i: (b, i, h)),
                  pl.BlockSpec((1, seq, LANES), lambda b, h, i: (b, 0, h)),
                  pl.BlockSpec((1, seq, LANES), lambda b, h, i: (b, 0, h)),
                  pl.BlockSpec((FOX_HEADS, 1, seq), lambda b, h, i: (0, 0, b)),
                  pl.BlockSpec((1, tq, LANES), lambda b, h, i: (b, i, og_blk + h)),
                  pl.BlockSpec((1, 1), lambda b, h, i: (0, 0))],
        out_specs=pl.BlockSpec((1, tq, LANES), lambda b, h, i: (b, i, h)),
        scratch_shapes=[pltpu.VMEM((2, tq, LANES), F32), pltpu.VMEM((2, tq, 2 * LANES), F32)],
        compiler_params=_cparams("parallel", "parallel", "arbitrary"),
        name="fox_attention")(qn.reshape(bsz, seq, D_MODEL), kn.reshape(bsz, seq, D_MODEL),
                              vog, crow.reshape(FOX_HEADS, 1, bsz * seq), vog, theta)


def _layer_norm(z, g, b):
    mu = jnp.mean(z, axis=-1, keepdims=True)
    zc = z - mu
    var = jnp.mean(zc * zc, axis=-1, keepdims=True)
    return zc * lax.rsqrt(var + LN_EPS) * g + b


def _block_tail_kernel(mix_ref, mem_ref, wo_ref, x_ref, g1_ref, b1_ref, wu_ref, wd_ref, g2_ref, b2_ref,
                       o_ref, ob_ref, *, fchunk):
    y = jnp.dot(mix_ref[...], wo_ref[0:D_MODEL, :], preferred_element_type=F32)
    y = y + jnp.dot(mem_ref[...], wo_ref[D_MODEL:D_MODEL + MEM_W, :], preferred_element_type=F32)
    x1 = _layer_norm(ALPHA * x_ref[...] + y, g1_ref[...], b1_ref[...])
    xb = x1.astype(BF16)
    y = jnp.zeros(x_ref.shape, F32)
    for c in range(D_FF // fchunk):
        h = jnp.dot(xb, wu_ref[:, c * fchunk:(c + 1) * fchunk], preferred_element_type=F32)
        h = jnp.square(jnp.maximum(h, 0.0)).astype(BF16)
        y = y + jnp.dot(h, wd_ref[c * fchunk:(c + 1) * fchunk, :], preferred_element_type=F32)
    out = _layer_norm(ALPHA * x1 + y, g2_ref[...], b2_ref[...])
    o_ref[...] = out
    ob_ref[...] = out.astype(ob_ref.dtype)


def _block_tail(mix, memo, wo, x, g1, b1, wu, wd, g2, b2, tm, fchunk):
    t = x.shape[0]
    tm = min(tm, t)
    row = lambda n: pl.BlockSpec((tm, n), lambda i: (i, 0))
    const = lambda s: pl.BlockSpec(s, lambda i: (0, 0), pipeline_mode=pl.Buffered(1))
    vec = const((1, D_MODEL))
    kern = functools.partial(_block_tail_kernel, fchunk=fchunk)
    return pl.pallas_call(
        kern,
        out_shape=(jax.ShapeDtypeStruct((t, D_MODEL), F32), jax.ShapeDtypeStruct((t, D_MODEL), BF16)),
        grid=(t // tm,),
        in_specs=[row(D_MODEL), row(MEM_W), const((D_MODEL + MEM_W, D_MODEL)), row(D_MODEL), vec, vec,
                  const((D_MODEL, D_FF)), const((D_FF, D_MODEL)), vec, vec],
        out_specs=(row(D_MODEL), row(D_MODEL)),
        compiler_params=_cparams("parallel"),
        name="block_tail")(mix, memo, wo, x, g1[None, :], b1[None, :], wu, wd, g2[None, :], b2[None, :])


def kernel(x, mem, gdn_w_in, gdn_conv_w, gdn_a_log, gdn_dt_bias, gdn_norm_g, mlstm_w_in, mlstm_b_gate,
           mlstm_norm_g, fox_w_in, fox_b_f, fox_qk_g, mem_w_kv, w_out, ln1_g, ln1_b, w_up, w_down,
           ln2_g, ln2_b):
    bsz, seq, _ = x.shape
    t = bsz * seq
    n_mem = mem.shape[1]
    x = x.reshape(t, D_MODEL)
    xb = x.astype(BF16)
    memb = mem.reshape(bsz * n_mem, D_MODEL).astype(BF16)
    for i in range(DEPTH):
        kind, j = i % N_MIXERS, i // N_MIXERS
        if kind == 0:
            w_in, n_main, n_gate = gdn_w_in[j], 4 * D_MODEL, 2 * GDN_HEADS
        elif kind == 1:
            w_in, n_main, n_gate = mlstm_w_in[j], 3 * D_MODEL, 2 * MLSTM_HEADS
        else:
            w_in, n_main, n_gate = fox_w_in[j], 4 * D_MODEL, FOX_HEADS
        w_gate = w_in[:, n_main:n_main + n_gate]
        w_memq = w_in[:, n_main + n_gate:].astype(BF16)
        proj_in = lambda lo, hi, dt: _matmul(xb, w_in[:, lo:hi].astype(BF16), dt, 1024, 1024, "in_proj")
        kv = _matmul(memb, mem_w_kv[i].astype(BF16), BF16, 512, 1024, "mem_kv")
        memo = _mem_attention(xb, w_memq, kv.reshape(bsz, n_mem, 2 * MEM_W), bsz, seq, 512)
        if kind == 0:
            grow, gsplit = _gates(xb, w_gate, gdn_a_log[j], gdn_dt_bias[j], "gdn", bsz, seq, 512)
            mix = _gdn_mixer(proj_in(0, GDN_QKV, F32), proj_in(GDN_QKV, n_main, BF16), gsplit, grow,
                             gdn_conv_w[j], gdn_norm_g[j], bsz, seq, 512, 4)
        elif kind == 1:
            bias = jnp.concatenate([mlstm_b_gate[j, 0], mlstm_b_gate[j, 1]])
            grow, gsplit = _gates(xb, w_gate, bias, bias, "mlstm", bsz, seq, 512)
            mix = _mlstm_mixer(proj_in(0, n_main, BF16), gsplit, grow, mlstm_norm_g[j], bsz, seq, 512)
        else:
            crow = _gates(xb, w_gate, fox_b_f[j], fox_b_f[j], "fox", bsz, seq, 512)
            qn, kn = _fox_prep(proj_in(0, 2 * D_MODEL, F32), fox_qk_g[j], 512)
            mix = _fox_attention(qn, kn, proj_in(2 * D_MODEL, n_main, BF16), crow, fox_qk_g[j], bsz, seq, 512, 2)
        x, xb = _block_tail(mix.reshape(t, D_MODEL), memo, w_out[i].astype(BF16), x, ln1_g[i], ln1_b[i],
                            w_up[i].astype(BF16), w_down[i].astype(BF16), ln2_g[i], ln2_b[i], 512, 1024)
    return x.reshape(bsz, seq, D_MODEL)
```

```python
import functools
import math

import jax
import jax.numpy as jnp
from jax import lax
from jax.experimental import pallas as pl
from jax.experimental.pallas import tpu as pltpu

F32 = jnp.float32
BF16 = jnp.bfloat16

D_MODEL = 1024
DEPTH = 4
N_MIXERS = 3
GDN_HEADS = 8
GDN_D = 128
GDN_QKV = 3 * GDN_HEADS * GDN_D
CONV_K = 4
MLSTM_HEADS = 8
MLSTM_DQK = 64
MLSTM_DV = 128
FOX_HEADS = 16
FOX_DH = 64
MEM_HEADS = 4
MEM_DH = 128
MEM_W = MEM_HEADS * MEM_DH
D_FF = 4 * D_MODEL
ALPHA = (2 * DEPTH) ** 0.25
LN_EPS = 1e-5
NORM_EPS = 1e-6

LANES = 128
SUBLANES = 8
VMEM_LIMIT = 56 * 1024 * 1024

CHUNK = 64
N_GATE_ROWS = 16
N_SPLIT = 3
INV_BLOCK = 16
NEG_INF = float("-inf")
LOG2E = math.log2(math.e)
FOX_UNDERFLOW = 160.0


def _cparams(*sem):
    return pltpu.CompilerParams(dimension_semantics=sem, vmem_limit_bytes=VMEM_LIMIT)


def _dot(a, b):
    return jnp.dot(a.astype(BF16), b.astype(BF16), preferred_element_type=F32)


def _dot_nt(a, b):
    return lax.dot_general(a.astype(BF16), b.astype(BF16), (((1,), (1,)), ((), ())),
                           preferred_element_type=F32)


def _dot_tn(a, b):
    return lax.dot_general(a.astype(BF16), b.astype(BF16), (((0,), (0,)), ((), ())),
                           preferred_element_type=F32)


def _sigmoid(x):
    return 1.0 / (1.0 + jnp.exp(-x))


def _softplus(x):
    return jnp.maximum(x, 0.0) + jnp.log1p(jnp.exp(-jnp.abs(x)))


def _iota2(shape, dim):
    return lax.broadcasted_iota(jnp.int32, shape, dim)


def _mm_kernel(x_ref, w_ref, o_ref):
    o_ref[...] = jnp.dot(x_ref[...], w_ref[...], preferred_element_type=F32).astype(o_ref.dtype)


def _matmul(x, w, out_dtype, tm, tn, name):
    m, k = x.shape
    n = w.shape[1]
    tm, tn = min(tm, m), min(tn, n)
    assert m % tm == 0 and n % tn == 0
    return pl.pallas_call(
        _mm_kernel,
        out_shape=jax.ShapeDtypeStruct((m, n), out_dtype),
        grid=(m // tm, n // tn),
        in_specs=[pl.BlockSpec((tm, k), lambda i, j: (i, 0)),
                  pl.BlockSpec((k, tn), lambda i, j: (0, j))],
        out_specs=pl.BlockSpec((tm, tn), lambda i, j: (i, j)),
        compiler_params=_cparams("parallel", "parallel"),
        name=name)(x, w)


PROJ_SUB = 256


def _proj_conv_kernel(x_ref, xh_ref, w_ref, cw_ref, o_ref, ext_ref, *, seq, l2norm, scale):
    tm, tn = o_ref.shape
    halo = SUBLANES
    starts_sequence = (pl.program_id(0) * tm) % seq == 0
    x = x_ref[...]
    xh = xh_ref[...]
    for c in range(tn // PROJ_SUB):
        cs = slice(c * PROJ_SUB, (c + 1) * PROJ_SUB)
        ext_ref[halo:halo + tm, cs] = jnp.dot(x, w_ref[:, cs], preferred_element_type=F32)
        hist = jnp.dot(xh, w_ref[:, cs], preferred_element_type=F32)
        ext_ref[0:halo, cs] = jnp.where(starts_sequence, 0.0, hist)
        for b in range(PROJ_SUB // LANES):
            cols = slice(c * PROJ_SUB + b * LANES, c * PROJ_SUB + (b + 1) * LANES)
            acc = cw_ref[CONV_K - 1:CONV_K, cols] * ext_ref[halo:halo + tm, cols]
            for tap in range(CONV_K - 1):
                off = halo - (CONV_K - 1) + tap
                acc = acc + cw_ref[tap:tap + 1, cols] * ext_ref[off:off + tm, cols]
            y = acc * _sigmoid(acc)
            if l2norm:
                y = y * lax.rsqrt(jnp.sum(y * y, axis=-1, keepdims=True) + NORM_EPS)
            if scale != 1.0:
                y = y * scale
            o_ref[:, cols] = y


def _proj_conv(xb, w, cw, seq, tm, l2norm, scale):
    t, kdim = xb.shape
    n = w.shape[1]
    tm = min(tm, seq)
    assert seq % tm == 0 and n % PROJ_SUB == 0
    kern = functools.partial(_proj_conv_kernel, seq=seq, l2norm=l2norm, scale=scale)
    const = lambda s: pl.BlockSpec(s, lambda i: (0, 0))
    return pl.pallas_call(
        kern,
        out_shape=jax.ShapeDtypeStruct((t, n), F32),
        grid=(t // tm,),
        in_specs=[pl.BlockSpec((tm, kdim), lambda i: (i, 0)),
                  pl.BlockSpec((SUBLANES, kdim), lambda i: (jnp.maximum(i * (tm // SUBLANES) - 1, 0), 0)),
                  const((kdim, n)), const((CONV_K, n))],
        out_specs=pl.BlockSpec((tm, n), lambda i: (i, 0)),
        scratch_shapes=[pltpu.VMEM((tm + SUBLANES, n), F32)],
        compiler_params=_cparams("parallel"),
        name="proj_conv")(xb, xb, w, cw)


def _proj_headnorm_kernel(x_ref, w_ref, g_ref, o_ref):
    tm, tn = o_ref.shape
    half = _iota2((tm, LANES), 1) < FOX_DH
    x = x_ref[...]
    for c in range(tn // PROJ_SUB):
        raw = jnp.dot(x, w_ref[:, c * PROJ_SUB:(c + 1) * PROJ_SUB], preferred_element_type=F32)
        for b in range(PROJ_SUB // LANES):
            cols = slice(c * PROJ_SUB + b * LANES, c * PROJ_SUB + (b + 1) * LANES)
            y = raw[:, b * LANES:(b + 1) * LANES]
            sq = y * y
            s_lo = jnp.sum(jnp.where(half, sq, 0.0), axis=-1, keepdims=True)
            s_all = jnp.sum(sq, axis=-1, keepdims=True)
            ms = jnp.where(half, s_lo, s_all - s_lo) * (1.0 / FOX_DH)
            o_ref[:, cols] = (y * lax.rsqrt(ms + NORM_EPS) * g_ref[:, cols]).astype(o_ref.dtype)


def _proj_headnorm(xb, w, gain, tm):
    t, kdim = xb.shape
    n = w.shape[1]
    tm = min(tm, t)
    const = lambda s: pl.BlockSpec(s, lambda i: (0, 0))
    return pl.pallas_call(
        _proj_headnorm_kernel,
        out_shape=jax.ShapeDtypeStruct((t, n), BF16),
        grid=(t // tm,),
        in_specs=[pl.BlockSpec((tm, kdim), lambda i: (i, 0)), const((kdim, n)), const((1, n))],
        out_specs=pl.BlockSpec((tm, n), lambda i: (i, 0)),
        compiler_params=_cparams("parallel"),
        name="proj_headnorm")(xb, w, gain[None, :].astype(F32))


def _scan_rows(x, seg):
    pos = _iota2(x.shape, 0) & (seg - 1)
    d = 1
    while d < seg:
        x = x + jnp.where(pos >= d, pltpu.roll(x, d, 0), 0.0)
        d *= 2
    return x


def _scan_lanes(x, seg):
    pos = _iota2(x.shape, 1) & (seg - 1)
    d = 1
    while d < seg:
        x = x + jnp.where(pos >= d, pltpu.roll(x, d, 1), 0.0)
        d *= 2
    return x


def _gate_act(kind, y, p1, p2, idx):
    if kind == "gdn":
        g = -jnp.exp(p1) * _softplus(y + p2)
        val = jnp.where(idx < GDN_HEADS, g, _sigmoid(y))
        scan = idx < GDN_HEADS
    elif kind == "mlstm":
        z = y + p2
        val = jnp.where(idx < MLSTM_HEADS, z, -_softplus(-z))
        scan = idx >= MLSTM_HEADS
    else:
        val = -_softplus(-(y + p2))
        scan = idx >= 0
    return val, scan


def _gates_kernel(x_ref, wg_ref, wgt_ref, p1_ref, p2_ref, p1t_ref, p2t_ref, row_ref, *rest, kind, seg):
    full = kind == "fox"
    tg = x_ref.shape[0]
    x = x_ref[...]
    yrow = lax.dot_general(wgt_ref[...], x, (((1,), (1,)), ((), ())),
                           preferred_element_type=F32)
    vrow, srow = _gate_act(kind, yrow, p1t_ref[...], p2t_ref[...], _iota2(yrow.shape, 0))
    crow = _scan_lanes(vrow, tg if full else seg)
    if full:
        (carry_ref,) = rest

        @pl.when(pl.program_id(1) == 0)
        def _():
            carry_ref[...] = jnp.zeros_like(carry_ref)
        crow = crow + carry_ref[...]
        carry_ref[...] = crow[:, tg - 1:tg]
        row_ref[...] = crow[:N_GATE_ROWS, :]
        return
    split_ref, _ = rest
    row_ref[...] = jnp.where(srow, crow, vrow)[:N_GATE_ROWS, :]
    ycol = jnp.dot(x, wg_ref[...], preferred_element_type=F32)
    vcol, scol = _gate_act(kind, ycol, p1_ref[...], p2_ref[...], _iota2(ycol.shape, 1))
    col = jnp.where(scol, _scan_rows(vcol, seg), vcol)
    hi = col.astype(BF16)
    r1 = col - hi.astype(F32)
    mid = r1.astype(BF16)
    lo = (r1 - mid.astype(F32)).astype(BF16)
    split_ref[:, 0:LANES] = hi
    split_ref[:, LANES:2 * LANES] = mid
    split_ref[:, 2 * LANES:3 * LANES] = lo


def _gates(xb, wg, p1, p2, kind, bsz, seq, tg):
    ng = wg.shape[1]
    wgp = jnp.pad(wg, ((0, 0), (0, LANES - ng))).astype(BF16)
    p1p = jnp.pad(p1, (0, LANES - p1.shape[0])).astype(F32)
    p2p = jnp.pad(p2, (0, LANES - p2.shape[0])).astype(F32)
    tg = min(tg, seq)
    nblk = seq // tg
    t = bsz * seq
    kern = functools.partial(_gates_kernel, kind=kind, seg=CHUNK)
    const = lambda b, s: (0, 0)
    row_shape = jax.ShapeDtypeStruct((N_GATE_ROWS, t), F32)
    row_spec = pl.BlockSpec((N_GATE_ROWS, tg), lambda b, s: (0, b * nblk + s))
    if kind == "fox":
        out_shape, out_specs = row_shape, row_spec
    else:
        out_shape = (row_shape, jax.ShapeDtypeStruct((t, N_SPLIT * LANES), BF16))
        out_specs = (row_spec, pl.BlockSpec((tg, N_SPLIT * LANES), lambda b, s: (b * nblk + s, 0)))
    return pl.pallas_call(
        kern,
        out_shape=out_shape,
        grid=(bsz, nblk),
        in_specs=[pl.BlockSpec((tg, D_MODEL), lambda b, s: (b * nblk + s, 0)),
                  pl.BlockSpec((D_MODEL, LANES), const),
                  pl.BlockSpec((LANES, D_MODEL), const),
                  pl.BlockSpec((1, LANES), const), pl.BlockSpec((1, LANES), const),
                  pl.BlockSpec((LANES, 1), const), pl.BlockSpec((LANES, 1), const)],
        out_specs=out_specs,
        scratch_shapes=[pltpu.VMEM((LANES, 1), F32)],
        compiler_params=_cparams("parallel", "arbitrary"),
        name=f"gates_{kind}")(xb, wgp, wgp.T, p1p[None, :], p2p[None, :], p1p[:, None], p2p[:, None])


def _gate_selectors(n_heads):
    r = jnp.arange(N_SPLIT * LANES)[None, :, None] % LANES
    c = jnp.arange(2 * LANES)[None, None, :]
    h = jnp.arange(n_heads)[:, None, None]
    return jnp.where(c < LANES, r == h, r == n_heads + h).astype(BF16)


def _memattn_kernel(x_ref, wq_ref, kv_ref, o_ref):
    q = jnp.dot(x_ref[...], wq_ref[...], preferred_element_type=F32).astype(BF16)
    scale = MEM_DH ** -0.5
    heads = range(MEM_HEADS)
    hc = [slice(h * MEM_DH, (h + 1) * MEM_DH) for h in heads]
    ones = jnp.ones((kv_ref.shape[1], LANES), BF16)
    logits = [_dot_nt(q[:, hc[h]], kv_ref[0, :, hc[h]]) * scale for h in heads]
    e = [jnp.exp(lg - jnp.max(lg, axis=-1, keepdims=True)) for lg in logits]
    pv = [_dot(e[h], jnp.concatenate([kv_ref[0, :, MEM_W + h * MEM_DH:MEM_W + (h + 1) * MEM_DH], ones], axis=1))
          for h in heads]
    for h in heads:
        o_ref[:, hc[h]] = (pv[h][:, :MEM_DH] / pv[h][:, MEM_DH:]).astype(o_ref.dtype)


def _mem_attention(xb, wq, kv, bsz, seq, tm):
    tm = min(tm, seq)
    nblk = seq // tm
    n_mem = kv.shape[1]
    return pl.pallas_call(
        _memattn_kernel,
        out_shape=jax.ShapeDtypeStruct((bsz * seq, MEM_W), BF16),
        grid=(bsz, nblk),
        in_specs=[pl.BlockSpec((tm, D_MODEL), lambda b, s: (b * nblk + s, 0)),
                  pl.BlockSpec((D_MODEL, MEM_W), lambda b, s: (0, 0)),
                  pl.BlockSpec((1, n_mem, 2 * MEM_W), lambda b, s: (b, 0, 0))],
        out_specs=pl.BlockSpec((tm, MEM_W), lambda b, s: (b * nblk + s, 0)),
        compiler_params=_cparams("parallel", "parallel"),
        name="mem_attention")(xb, wq, kv)


def _unit_lower_inverse(a_list):
    c = a_list[0].shape[0]
    ri, ci = _iota2((c, c), 0), _iota2((c, c), 1)
    eye = (ri == ci).astype(F32)
    shift = int(math.log2(INV_BLOCK))
    same_block = (ri >> shift) == (ci >> shift)
    d = [jnp.where(same_block, a, 0.0) for a in a_list]
    n = [a - di for a, di in zip(a_list, d)]
    t = [eye - di for di in d]
    p = d
    for _ in range(shift - 1):
        p = [_dot(pi, pi) for pi in p]
        t = [ti + _dot(ti, pi) for ti, pi in zip(t, p)]
    nb = c // INV_BLOCK
    if nb > 1:
        m = [_dot(ti, ni) for ti, ni in zip(t, n)]
        x = [eye - mi for mi in m]
        p = m
        for _ in range(int(math.log2(nb)) - 1):
            p = [_dot(pi, pi) for pi in p]
            x = [xi + _dot(xi, pi) for xi, pi in zip(x, p)]
        t = [_dot(xi, ti) for xi, ti in zip(x, t)]
    return t


def _gdn_kernel(q_ref, k_ref, v_ref, z_ref, gs_ref, grow_ref, sel_ref, ng_ref, o_ref,
                u_ref, w_ref, qg_ref, kg_ref, attn_ref, dl_ref, st_ref, *, chunk, group):
    ts = q_ref.shape[1]
    nchunk = ts // chunk
    s_idx = pl.program_id(1)
    blk = pl.program_id(0) * pl.num_programs(1) + s_idx

    @pl.when(s_idx == 0)
    def _():
        st_ref[...] = jnp.zeros_like(st_ref)

    ri, ci = _iota2((chunk, chunk), 0), _iota2((chunk, chunk), 1)
    incl = ri >= ci
    strict = ri > ci

    def intra(gi, carry):
        items = [(cc, h) for cc in range(group) for h in range(GDN_HEADS)]
        cidx = [gi * group + cc for cc in range(group)]
        rows = [pl.ds(pl.multiple_of(c * chunk, chunk), chunk) for c in cidx]
        gsp = [gs_ref[r, :] for r in rows]
        gb = [_dot(gsp[cc], sel_ref[h]) for cc, h in items]
        gcc = [g[:, :LANES] for g in gb]
        bc = [g[:, LANES:] for g in gb]
        q = [q_ref[0, rows[cc], h * GDN_D:(h + 1) * GDN_D] for cc, h in items]
        k = [k_ref[0, rows[cc], h * GDN_D:(h + 1) * GDN_D] for cc, h in items]
        v = [v_ref[0, rows[cc], h * GDN_D:(h + 1) * GDN_D] for cc, h in items]
        decay = [jnp.exp(jnp.where(incl, gcc[n][:, :chunk]
                                   - grow_ref[h, pl.ds(blk * nchunk + cidx[cc], 1), :], NEG_INF))
                 for n, (cc, h) in enumerate(items)]
        kb = [ki * bi for ki, bi in zip(k, bc)]
        kq = [_dot_nt(jnp.concatenate([kbi, qi], axis=0), ki) for kbi, qi, ki in zip(kb, q, k)]
        a = [jnp.where(strict, kqi[:chunk] * di, 0.0) for kqi, di in zip(kq, decay)]
        tinv = _unit_lower_inverse(a)
        egc = [jnp.exp(g) for g in gcc]
        sol = [_dot(ti, jnp.concatenate([vi * bi, kbi * ei], axis=1))
               for ti, vi, bi, kbi, ei in zip(tinv, v, bc, kb, egc)]
        for n, (cc, h) in enumerate(items):
            hc = slice(h * GDN_D, (h + 1) * GDN_D)
            glast = gcc[n][chunk - 1:chunk, :]
            u_ref[rows[cc], hc] = sol[n][:, :GDN_D]
            w_ref[rows[cc], hc] = sol[n][:, GDN_D:].astype(BF16)
            qg_ref[rows[cc], hc] = (q[n] * egc[n]).astype(BF16)
            kg_ref[rows[cc], hc] = (k[n] * jnp.exp(glast - gcc[n])).astype(BF16)
            attn_ref[h, rows[cc], :] = (kq[n][chunk:] * decay[n]).astype(BF16)
            dl_ref[h, pl.ds(cidx[cc], 1), :] = jnp.exp(glast)
        return carry

    lax.fori_loop(0, nchunk // group, intra, 0)

    def inter(c, carry):
        rows = pl.ds(pl.multiple_of(c * chunk, chunk), chunk)
        heads = range(GDN_HEADS)
        hcs = [slice(h * GDN_D, (h + 1) * GDN_D) for h in heads]
        ng = ng_ref[...]
        state = [st_ref[h] for h in heads]
        wq = [_dot(jnp.concatenate([w_ref[rows, hcs[h]], qg_ref[rows, hcs[h]]], axis=0), state[h])
              for h in heads]
        v_new = [(u_ref[rows, hcs[h]] - wq[h][:chunk]).astype(BF16) for h in heads]
        o2 = [_dot(attn_ref[h, rows, :], v_new[h]) for h in heads]
        upd = [_dot_tn(kg_ref[rows, hcs[h]], v_new[h]) for h in heads]
        for h in heads:
            st_ref[h] = state[h] * dl_ref[h, pl.ds(c, 1), :] + upd[h]
        for h in heads:
            o = wq[h][chunk:] + o2[h]
            z = z_ref[0, rows, hcs[h]].astype(F32)
            on = o * lax.rsqrt(jnp.mean(o * o, axis=-1, keepdims=True) + NORM_EPS) * ng
            o_ref[0, rows, hcs[h]] = (on * (z * _sigmoid(z))).astype(o_ref.dtype)
        return carry

    lax.fori_loop(0, nchunk, inter, 0)


def _gdn_mixer(q, k, v, zproj, gsplit, grow, norm_g, bsz, seq, ts, group):
    ts = min(ts, seq)
    nblk = seq // ts
    t = bsz * seq
    nchunk = ts // CHUNK
    kern = functools.partial(_gdn_kernel, chunk=CHUNK, group=min(group, nchunk))
    grow3 = grow.reshape(grow.shape[0], t // CHUNK, CHUNK)
    sel = _gate_selectors(GDN_HEADS)
    full = lambda a: pl.BlockSpec(a.shape, lambda b, s: (0,) * a.ndim, pipeline_mode=pl.Buffered(1))
    blk = pl.BlockSpec((1, ts, D_MODEL), lambda b, s: (b, s, 0))
    seq3 = lambda a: a.reshape(bsz, seq, D_MODEL)
    return pl.pallas_call(
        kern,
        out_shape=jax.ShapeDtypeStruct((bsz, seq, D_MODEL), BF16),
        grid=(bsz, nblk),
        in_specs=[blk, blk, blk, blk,
                  pl.BlockSpec((ts, N_SPLIT * LANES), lambda b, s: (b * nblk + s, 0)),
                  full(grow3), full(sel),
                  pl.BlockSpec((1, GDN_D), lambda b, s: (0, 0))],
        out_specs=blk,
        scratch_shapes=[pltpu.VMEM((ts, D_MODEL), F32),
                        pltpu.VMEM((ts, D_MODEL), BF16),
                        pltpu.VMEM((ts, D_MODEL), BF16),
                        pltpu.VMEM((ts, D_MODEL), BF16),
                        pltpu.VMEM((GDN_HEADS, ts, CHUNK), BF16),
                        pltpu.VMEM((GDN_HEADS, nchunk, LANES), F32),
                        pltpu.VMEM((GDN_HEADS, GDN_D, GDN_D), F32)],
        compiler_params=_cparams("parallel", "arbitrary"),
        name="gdn_mixer")(seq3(q), seq3(k), seq3(v), seq3(zproj), gsplit, grow3, sel, norm_g[None, :])


def _mlstm_kernel(x_ref, gs_ref, grow_ref, sel_ref, ng_ref, o_ref, st_ref, m_ref, *, chunk):
    ts = x_ref.shape[1]
    nchunk = ts // chunk
    s_idx = pl.program_id(1)
    blk = pl.program_id(0) * pl.num_programs(1) + s_idx
    nh, dv = MLSTM_HEADS, MLSTM_DV
    k0, v0, og0 = nh * MLSTM_DQK, 2 * nh * MLSTM_DQK, 2 * nh * MLSTM_DQK + nh * dv

    @pl.when(s_idx == 0)
    def _():
        st_ref[...] = jnp.zeros_like(st_ref)
        m_ref[...] = jnp.zeros_like(m_ref)

    incl = _iota2((chunk, chunk), 0) >= _iota2((chunk, chunk), 1)
    lo = _iota2((chunk, LANES), 1) < MLSTM_DQK
    ones = jnp.ones((chunk, LANES), x_ref.dtype)
    twice = lambda r: jnp.concatenate([r, r], axis=1)

    def body(c, carry):
        rows = pl.ds(pl.multiple_of(c * chunk, chunk), chunk)
        cg = blk * nchunk + c
        heads = range(nh)
        gsp = gs_ref[rows, :]
        gb = [_dot(gsp, sel_ref[h]) for h in heads]
        irep = [g[:, :LANES] for g in gb]
        brep = [g[:, LANES:] for g in gb]
        qscale = jnp.asarray(MLSTM_DQK ** -0.5, x_ref.dtype)
        q2 = [x_ref[0, rows, p * LANES:(p + 1) * LANES] * qscale for p in range(nh // 2)]
        k2 = [x_ref[0, rows, k0 + p * LANES:k0 + (p + 1) * LANES] for p in range(nh // 2)]
        zero = jnp.zeros_like(q2[0])
        q = [jnp.where(lo if h % 2 == 0 else ~lo, q2[h // 2], zero) for h in heads]
        k = [jnp.where(lo if h % 2 == 0 else ~lo, k2[h // 2], zero) for h in heads]
        vext = [jnp.concatenate([x_ref[0, rows, v0 + h * dv:v0 + (h + 1) * dv], ones], axis=1)
                for h in heads]
        qk = [_dot_nt(q[h], k2[h // 2]) for h in heads]
        state = [st_ref[h] for h in heads]
        qs = [_dot(q[h], state[h]) for h in heads]
        logd = [jnp.where(incl, brep[h][:, :chunk] - grow_ref[nh + h, pl.ds(cg, 1), :]
                          + grow_ref[h, pl.ds(cg, 1), :], NEG_INF) for h in heads]
        m_intra = [jnp.max(ld, axis=-1, keepdims=True) for ld in logd]
        blast = [b[chunk - 1:chunk, :] for b in brep]
        lw = [blast[h] - brep[h] + irep[h] for h in heads]
        mw = [jnp.max(x, axis=0, keepdims=True) for x in lw]
        m_prev = [m_ref[h:h + 1, :] for h in heads]
        a_inter = [brep[h] + m_prev[h] for h in heads]
        m_t = [jnp.maximum(a_inter[h], m_intra[h]) for h in heads]
        p = [qk[h] * jnp.exp(logd[h] - m_t[h][:, :chunk]) for h in heads]
        pv = [_dot(p[h], vext[h]) for h in heads]
        m_new = [jnp.maximum(blast[h] + m_prev[h], mw[h]) for h in heads]
        for h in heads:
            kw = k[h] * jnp.exp(lw[h] - m_new[h])
            keep = jnp.exp(blast[h] + m_prev[h] - m_new[h])
            st_ref[h] = twice(keep) * state[h] + _dot_tn(kw, vext[h])
            m_ref[h:h + 1, :] = m_new[h]
        for h in heads:
            num = twice(jnp.exp(a_inter[h] - m_t[h])) * qs[h] + pv[h]
            hh = num[:, :dv] / jnp.maximum(jnp.abs(num[:, dv:]), jnp.exp(-m_t[h]))
            mu = jnp.mean(hh, axis=-1, keepdims=True)
            xc = hh - mu
            var = jnp.mean(xc * xc, axis=-1, keepdims=True)
            hn = xc * lax.rsqrt(var + NORM_EPS) * ng_ref[h:h + 1, :]
            og = x_ref[0, rows, og0 + h * dv:og0 + (h + 1) * dv].astype(F32)
            o_ref[0, rows, h * dv:(h + 1) * dv] = (hn * _sigmoid(og)).astype(o_ref.dtype)
        return carry

    lax.fori_loop(0, nchunk, body, 0)


def _mlstm_mixer(proj, gsplit, grow, norm_g, bsz, seq, ts):
    ts = min(ts, seq)
    nblk = seq // ts
    t = bsz * seq
    ncols = proj.shape[1]
    kern = functools.partial(_mlstm_kernel, chunk=CHUNK)
    grow3 = grow.reshape(grow.shape[0], t // CHUNK, CHUNK)
    sel = _gate_selectors(MLSTM_HEADS)
    full = lambda a: pl.BlockSpec(a.shape, lambda b, s: (0,) * a.ndim)
    return pl.pallas_call(
        kern,
        out_shape=jax.ShapeDtypeStruct((bsz, seq, D_MODEL), BF16),
        grid=(bsz, nblk),
        in_specs=[pl.BlockSpec((1, ts, ncols), lambda b, s: (b, s, 0)),
                  pl.BlockSpec((ts, N_SPLIT * LANES), lambda b, s: (b * nblk + s, 0)),
                  full(grow3), full(sel),
                  pl.BlockSpec((MLSTM_HEADS, MLSTM_DV), lambda b, s: (0, 0))],
        out_specs=pl.BlockSpec((1, ts, D_MODEL), lambda b, s: (b, s, 0)),
        scratch_shapes=[pltpu.VMEM((MLSTM_HEADS, LANES, 2 * LANES), F32),
                        pltpu.VMEM((MLSTM_HEADS, LANES), F32)],
        compiler_params=_cparams("parallel", "arbitrary"),
        name="mlstm_mixer")(proj.reshape(bsz, seq, ncols), gsplit, grow3, sel, norm_g)


def _fox_kernel(q_ref, k_ref, v_ref, crow_ref, og_ref, theta_ref, o_ref, m_ref, acc_ref, *, tq, wide):
    i = pl.program_id(2)
    hp = pl.program_id(1)
    pair = range(2)
    seq = k_ref.shape[1]
    lo = _iota2((tq, LANES), 1) < FOX_DH
    q = q_ref[0]
    zero = jnp.zeros_like(q)
    qs = (jnp.where(lo, q, zero), jnp.where(lo, zero, q))
    q0 = pl.multiple_of(i * tq, tq)
    cref = [crow_ref[2 * hp + e, :, pl.ds(q0, LANES)][:, 0:1] for e in pair]
    pos = _iota2((1, seq), 1)
    first_live = [jnp.min(jnp.where(cref[e] - crow_ref[2 * hp + e] < -theta_ref[...], seq, pos)) for e in pair]
    n_dead = jnp.minimum(first_live[0], first_live[1])
    m_ref[...] = jnp.full(m_ref.shape, NEG_INF, F32)
    acc_ref[...] = jnp.zeros_like(acc_ref)
    causal = _iota2((tq, tq), 0) >= _iota2((tq, tq), 1)

    def step(k0, width, masked):
        k = k_ref[0, pl.ds(k0, width), :]
        vext = jnp.concatenate([v_ref[0, pl.ds(k0, width), :], jnp.ones((width, LANES), BF16)], axis=1)
        s = [lax.dot_general(qs[e], k, (((1,), (1,)), ((), ())), preferred_element_type=F32) for e in pair]
        for e in pair:
            bias = (cref[e] - crow_ref[2 * hp + e, :, pl.ds(k0, width)]) * LOG2E
            s[e] = s[e] + bias
            if masked:
                s[e] = jnp.where(causal, s[e], NEG_INF)
        m_prev = [m_ref[e] for e in pair]
        m_new = [jnp.maximum(m_prev[e], jnp.max(s[e], axis=-1, keepdims=True)) for e in pair]
        p = [jnp.exp2(s[e] - jnp.concatenate([m_new[e]] * (width // LANES), axis=1)) for e in pair]
        pv = [jnp.dot(p[e].astype(BF16), vext, preferred_element_type=F32) for e in pair]
        for e in pair:
            alpha = jnp.exp2(m_prev[e] - m_new[e])
            acc_ref[e] = jnp.concatenate([alpha, alpha], axis=1) * acc_ref[e] + pv[e]
            m_ref[e] = m_new[e]

    def body(j, carry):
        step(pl.multiple_of(j * (wide * tq), wide * tq), wide * tq, False)
        return carry

    lax.fori_loop(n_dead // (wide * tq), i // wide, body, 0)
    if wide == 2:
        @pl.when(jnp.logical_and(i % 2 == 1, q0 > n_dead))
        def _():
            step(pl.multiple_of((i - 1) * tq, tq), tq, False)
    step(q0, tq, True)
    out = jnp.where(lo, acc_ref[0, :, :LANES] / acc_ref[0, :, LANES:], acc_ref[1, :, :LANES] / acc_ref[1, :, LANES:])
    o_ref[0] = (out * _sigmoid(og_ref[0].astype(F32))).astype(o_ref.dtype)


def _fox_attention(qn, kn, vog, crow, qk_g, bsz, seq, tq, wide):
    tq = min(tq, seq)
    nq = seq // tq
    npair = FOX_HEADS // 2
    kern = functools.partial(_fox_kernel, tq=tq, wide=wide)
    og_blk = D_MODEL // LANES
    vog = vog.reshape(bsz, seq, 2 * D_MODEL)
    qk_bound = 1.02 * FOX_DH ** 0.5 * LOG2E * jnp.max(jnp.abs(qk_g[0])) * jnp.max(jnp.abs(qk_g[1]))
    theta = ((FOX_UNDERFLOW + 2.0 * qk_bound) / LOG2E).reshape(1, 1).astype(F32)
    return pl.pallas_call(
        kern,
        out_shape=jax.ShapeDtypeStruct((bsz, seq, D_MODEL), BF16),
        grid=(bsz, npair, nq),
        in_specs=[pl.BlockSpec((1, tq, LANES), lambda b, h, i: (b, i, h)),
                  pl.BlockSpec((1, seq, LANES), lambda b, h, i: (b, 0, h)),
                  pl.BlockSpec((1, seq, LANES), lambda b, h, i: (b, 0, h)),
                  pl.BlockSpec((FOX_HEADS, 1, seq), lambda b, h, i: (0, 0, b)),
                  pl.BlockSpec((1, tq, LANES), lambda b, h, i: (b, i, og_blk + h)),
                  pl.BlockSpec((1, 1), lambda b, h, i: (0, 0))],
        out_specs=pl.BlockSpec((1, tq, LANES), lambda b, h, i: (b, i, h)),
        scratch_shapes=[pltpu.VMEM((2, tq, LANES), F32), pltpu.VMEM((2, tq, 2 * LANES), F32)],
        compiler_params=_cparams("parallel", "parallel", "arbitrary"),
        name="fox_attention")(qn.reshape(bsz, seq, D_MODEL), kn.reshape(bsz, seq, D_MODEL),
                              vog, crow.reshape(FOX_HEADS, 1, bsz * seq), vog, theta)


def _layer_norm(z, g, b):
    mu = jnp.mean(z, axis=-1, keepdims=True)
    zc = z - mu
    var = jnp.mean(zc * zc, axis=-1, keepdims=True)
    return zc * lax.rsqrt(var + LN_EPS) * g + b


def _block_tail_kernel(mix_ref, mem_ref, wo_ref, x_ref, g1_ref, b1_ref, wu_ref, wd_ref, g2_ref, b2_ref,
                       o_ref, ob_ref, *, fchunk):
    y = jnp.dot(mix_ref[...], wo_ref[0:D_MODEL, :], preferred_element_type=F32)
    y = y + jnp.dot(mem_ref[...], wo_ref[D_MODEL:D_MODEL + MEM_W, :], preferred_element_type=F32)
    x1 = _layer_norm(ALPHA * x_ref[...] + y, g1_ref[...], b1_ref[...])
    xb = x1.astype(BF16)
    y = jnp.zeros(x_ref.shape, F32)
    for c in range(D_FF // fchunk):
        h = jnp.dot(xb, wu_ref[:, c * fchunk:(c + 1) * fchunk], preferred_element_type=F32)
        h = jnp.square(jnp.maximum(h, 0.0)).astype(BF16)
        y = y + jnp.dot(h, wd_ref[c * fchunk:(c + 1) * fchunk, :], preferred_element_type=F32)
    out = _layer_norm(ALPHA * x1 + y, g2_ref[...], b2_ref[...])
    o_ref[...] = out
    ob_ref[...] = out.astype(ob_ref.dtype)


def _block_tail(mix, memo, wo, x, g1, b1, wu, wd, g2, b2, tm, fchunk):
    t = x.shape[0]
    tm = min(tm, t)
    row = lambda n: pl.BlockSpec((tm, n), lambda i: (i, 0))
    const = lambda s: pl.BlockSpec(s, lambda i: (0, 0), pipeline_mode=pl.Buffered(1))
    vec = const((1, D_MODEL))
    kern = functools.partial(_block_tail_kernel, fchunk=fchunk)
    return pl.pallas_call(
        kern,
        out_shape=(jax.ShapeDtypeStruct((t, D_MODEL), F32), jax.ShapeDtypeStruct((t, D_MODEL), BF16)),
        grid=(t // tm,),
        in_specs=[row(D_MODEL), row(MEM_W), const((D_MODEL + MEM_W, D_MODEL)), row(D_MODEL), vec, vec,
                  const((D_MODEL, D_FF)), const((D_FF, D_MODEL)), vec, vec],
        out_specs=(row(D_MODEL), row(D_MODEL)),
        compiler_params=_cparams("parallel"),
        name="block_tail")(mix, memo, wo, x, g1[None, :], b1[None, :], wu, wd, g2[None, :], b2[None, :])


def kernel(x, mem, gdn_w_in, gdn_conv_w, gdn_a_log, gdn_dt_bias, gdn_norm_g, mlstm_w_in, mlstm_b_gate,
           mlstm_norm_g, fox_w_in, fox_b_f, fox_qk_g, mem_w_kv, w_out, ln1_g, ln1_b, w_up, w_down,
           ln2_g, ln2_b):
    bsz, seq, _ = x.shape
    t = bsz * seq
    n_mem = mem.shape[1]
    x = x.reshape(t, D_MODEL)
    xb = x.astype(BF16)
    memb = mem.reshape(bsz * n_mem, D_MODEL).astype(BF16)
    for i in range(DEPTH):
        kind, j = i % N_MIXERS, i // N_MIXERS
        if kind == 0:
            w_in, n_main, n_gate = gdn_w_in[j], 4 * D_MODEL, 2 * GDN_HEADS
        elif kind == 1:
            w_in, n_main, n_gate = mlstm_w_in[j], 3 * D_MODEL, 2 * MLSTM_HEADS
        else:
            w_in, n_main, n_gate = fox_w_in[j], 4 * D_MODEL, FOX_HEADS
        w_gate = w_in[:, n_main:n_main + n_gate]
        w_memq = w_in[:, n_main + n_gate:].astype(BF16)
        proj_in = lambda lo, hi, dt: _matmul(xb, w_in[:, lo:hi].astype(BF16), dt, 1024, 1024, "in_proj")
        kv = _matmul(memb, mem_w_kv[i].astype(BF16), BF16, 512, 1024, "mem_kv")
        memo = _mem_attention(xb, w_memq, kv.reshape(bsz, n_mem, 2 * MEM_W), bsz, seq, 512)
        wo = w_out[i]
        if kind == 0:
            grow, gsplit = _gates(xb, w_gate, gdn_a_log[j], gdn_dt_bias[j], "gdn", bsz, seq, 512)
            qkv = [_proj_conv(xb, w_in[:, n * D_MODEL:(n + 1) * D_MODEL].astype(BF16),
                              gdn_conv_w[j][:, n * D_MODEL:(n + 1) * D_MODEL], seq, 1024, n < 2,
                              GDN_D ** -0.5 if n == 0 else 1.0) for n in range(3)]
            mix = _gdn_mixer(*qkv, proj_in(GDN_QKV, n_main, BF16), gsplit, grow, gdn_norm_g[j], bsz, seq, 512, 4)
        elif kind == 1:
            bias = jnp.concatenate([mlstm_b_gate[j, 0], mlstm_b_gate[j, 1]])
            grow, gsplit = _gates(xb, w_gate, bias, bias, "mlstm", bsz, seq, 512)
            mix = _mlstm_mixer(proj_in(0, n_main, BF16), gsplit, grow, mlstm_norm_g[j], bsz, seq, 512)
        else:
            order = jnp.argsort(fox_b_f[j])
            cols = (order[:, None] * FOX_DH + jnp.arange(FOX_DH)[None, :]).reshape(-1)
            part = lambda n: w_in[:, n * D_MODEL + cols].astype(BF16)
            crow = _gates(xb, w_gate[:, order], fox_b_f[j][order], fox_b_f[j][order], "fox", bsz, seq, 512)
            qn = _proj_headnorm(xb, part(0), jnp.tile(fox_qk_g[j, 0], FOX_HEADS) * (FOX_DH ** -0.5 * LOG2E), 1024)
            kn = _proj_headnorm(xb, part(1), jnp.tile(fox_qk_g[j, 1], FOX_HEADS), 1024)
            vog = _matmul(xb, jnp.concatenate([part(2), part(3)], axis=1), BF16, 1024, 1024, "in_proj")
            mix = _fox_attention(qn, kn, vog, crow, fox_qk_g[j], bsz, seq, 512, 2)
            wo = jnp.concatenate([wo[:D_MODEL][cols], wo[D_MODEL:]], axis=0)
        x, xb = _block_tail(mix.reshape(t, D_MODEL), memo, wo.astype(BF16), x, ln1_g[i], ln1_b[i],
                            w_up[i].astype(BF16), w_down[i].astype(BF16), ln2_g[i], ln2_b[i], 512, 1024)
    return x.reshape(bsz, seq, D_MODEL)
```

```python
import functools
import math

import jax
import jax.numpy as jnp
from jax import lax
from jax.experimental import pallas as pl
from jax.experimental.pallas import tpu as pltpu

F32 = jnp.float32
BF16 = jnp.bfloat16

D_MODEL = 1024
DEPTH = 4
N_MIXERS = 3
GDN_HEADS = 8
GDN_D = 128
GDN_QKV = 3 * GDN_HEADS * GDN_D
CONV_K = 4
MLSTM_HEADS = 8
MLSTM_DQK = 64
MLSTM_DV = 128
FOX_HEADS = 16
FOX_DH = 64
MEM_HEADS = 4
MEM_DH = 128
MEM_W = MEM_HEADS * MEM_DH
D_FF = 4 * D_MODEL
ALPHA = (2 * DEPTH) ** 0.25
LN_EPS = 1e-5
NORM_EPS = 1e-6

LANES = 128
SUBLANES = 8
VMEM_LIMIT = 56 * 1024 * 1024

TM_PROJ = 1024
TN_PROJ = 1024
TM_SMALL = 1024
TS_MIXER = 512
GDN_GROUP = 4
TQ_FOX = 512
FOX_WIDE = 2
TM_TAIL = 512
F_CHUNK = 1024

CHUNK = 64
N_GATE_ROWS = 16
N_SPLIT = 3
INV_BLOCK = 16
NEG_INF = float("-inf")
LOG2E = math.log2(math.e)
FOX_UNDERFLOW = 160.0


def _cparams(*sem):
    return pltpu.CompilerParams(dimension_semantics=sem, vmem_limit_bytes=VMEM_LIMIT)


def _dot(a, b):
    return jnp.dot(a.astype(BF16), b.astype(BF16), preferred_element_type=F32)


def _dot_nt(a, b):
    return lax.dot_general(a.astype(BF16), b.astype(BF16), (((1,), (1,)), ((), ())),
                           preferred_element_type=F32)


def _dot_tn(a, b):
    return lax.dot_general(a.astype(BF16), b.astype(BF16), (((0,), (0,)), ((), ())),
                           preferred_element_type=F32)


def _sigmoid(x):
    return 1.0 / (1.0 + jnp.exp(-x))


def _softplus(x):
    return jnp.maximum(x, 0.0) + jnp.log1p(jnp.exp(-jnp.abs(x)))


def _iota2(shape, dim):
    return lax.broadcasted_iota(jnp.int32, shape, dim)


def _mm_kernel(x_ref, w_ref, o_ref):
    o_ref[...] = jnp.dot(x_ref[...], w_ref[...], preferred_element_type=F32).astype(o_ref.dtype)


def _matmul(x, w, out_dtype, tm, tn, name):
    m, k = x.shape
    n = w.shape[1]
    tm, tn = min(tm, m), min(tn, n)
    assert m % tm == 0 and n % tn == 0
    return pl.pallas_call(
        _mm_kernel,
        out_shape=jax.ShapeDtypeStruct((m, n), out_dtype),
        grid=(m // tm, n // tn),
        in_specs=[pl.BlockSpec((tm, k), lambda i, j: (i, 0)),
                  pl.BlockSpec((k, tn), lambda i, j: (0, j))],
        out_specs=pl.BlockSpec((tm, tn), lambda i, j: (i, j)),
        compiler_params=_cparams("parallel", "parallel"),
        name=name)(x, w)


PROJ_SUB = 256


def _proj_conv_kernel(x_ref, xh_ref, w_ref, cw_ref, o_ref, ext_ref, *, seq, l2norm, scale):
    tm, tn = o_ref.shape
    halo = SUBLANES
    starts_sequence = (pl.program_id(0) * tm) % seq == 0
    x = x_ref[...]
    xh = xh_ref[...]
    for c in range(tn // PROJ_SUB):
        cs = slice(c * PROJ_SUB, (c + 1) * PROJ_SUB)
        ext_ref[halo:halo + tm, cs] = jnp.dot(x, w_ref[:, cs], preferred_element_type=F32)
        hist = jnp.dot(xh, w_ref[:, cs], preferred_element_type=F32)
        ext_ref[0:halo, cs] = jnp.where(starts_sequence, 0.0, hist)
        for b in range(PROJ_SUB // LANES):
            cols = slice(c * PROJ_SUB + b * LANES, c * PROJ_SUB + (b + 1) * LANES)
            acc = cw_ref[CONV_K - 1:CONV_K, cols] * ext_ref[halo:halo + tm, cols]
            for tap in range(CONV_K - 1):
                off = halo - (CONV_K - 1) + tap
                acc = acc + cw_ref[tap:tap + 1, cols] * ext_ref[off:off + tm, cols]
            y = acc * _sigmoid(acc)
            if l2norm:
                y = y * (lax.rsqrt(jnp.sum(y * y, axis=-1, keepdims=True) + NORM_EPS) * scale)
            o_ref[:, cols] = y


def _proj_conv(xb, w, cw, seq, tm, l2norm, scale):
    t, kdim = xb.shape
    n = w.shape[1]
    tm = min(tm, seq)
    assert seq % tm == 0 and n % PROJ_SUB == 0
    assert l2norm or scale == 1.0
    kern = functools.partial(_proj_conv_kernel, seq=seq, l2norm=l2norm, scale=scale)
    const = lambda s: pl.BlockSpec(s, lambda i: (0, 0))
    return pl.pallas_call(
        kern,
        out_shape=jax.ShapeDtypeStruct((t, n), F32),
        grid=(t // tm,),
        in_specs=[pl.BlockSpec((tm, kdim), lambda i: (i, 0)),
                  pl.BlockSpec((SUBLANES, kdim), lambda i: (jnp.maximum(i * (tm // SUBLANES) - 1, 0), 0)),
                  const((kdim, n)), const((CONV_K, n))],
        out_specs=pl.BlockSpec((tm, n), lambda i: (i, 0)),
        scratch_shapes=[pltpu.VMEM((tm + SUBLANES, n), F32)],
        compiler_params=_cparams("parallel"),
        name="proj_conv")(xb, xb, w, cw)


def _proj_headnorm_kernel(x_ref, w_ref, g_ref, o_ref):
    tm, tn = o_ref.shape
    half = _iota2((tm, LANES), 1) < FOX_DH
    x = x_ref[...]
    for c in range(tn // PROJ_SUB):
        raw = jnp.dot(x, w_ref[:, c * PROJ_SUB:(c + 1) * PROJ_SUB], preferred_element_type=F32)
        for b in range(PROJ_SUB // LANES):
            cols = slice(c * PROJ_SUB + b * LANES, c * PROJ_SUB + (b + 1) * LANES)
            y = raw[:, b * LANES:(b + 1) * LANES]
            sq = y * y
            s_lo = jnp.sum(jnp.where(half, sq, 0.0), axis=-1, keepdims=True)
            s_all = jnp.sum(sq, axis=-1, keepdims=True)
            ms = jnp.where(half, s_lo, s_all - s_lo) * (1.0 / FOX_DH)
            o_ref[:, cols] = (y * lax.rsqrt(ms + NORM_EPS) * g_ref[:, cols]).astype(o_ref.dtype)


def _proj_headnorm(xb, w, gain, tm):
    t, kdim = xb.shape
    n = w.shape[1]
    tm = min(tm, t)
    const = lambda s: pl.BlockSpec(s, lambda i: (0, 0))
    return pl.pallas_call(
        _proj_headnorm_kernel,
        out_shape=jax.ShapeDtypeStruct((t, n), BF16),
        grid=(t // tm,),
        in_specs=[pl.BlockSpec((tm, kdim), lambda i: (i, 0)), const((kdim, n)), const((1, n))],
        out_specs=pl.BlockSpec((tm, n), lambda i: (i, 0)),
        compiler_params=_cparams("parallel"),
        name="proj_headnorm")(xb, w, gain[None, :].astype(F32))


def _scan_rows(x, seg):
    pos = _iota2(x.shape, 0) & (seg - 1)
    d = 1
    while d < seg:
        x = x + jnp.where(pos >= d, pltpu.roll(x, d, 0), 0.0)
        d *= 2
    return x


def _scan_lanes(x, seg):
    pos = _iota2(x.shape, 1) & (seg - 1)
    d = 1
    while d < seg:
        x = x + jnp.where(pos >= d, pltpu.roll(x, d, 1), 0.0)
        d *= 2
    return x


def _gate_act(kind, y, p1, p2, idx):
    if kind == "gdn":
        g = -jnp.exp(p1) * _softplus(y + p2)
        val = jnp.where(idx < GDN_HEADS, g, _sigmoid(y))
        scan = idx < GDN_HEADS
    elif kind == "mlstm":
        z = y + p2
        val = jnp.where(idx < MLSTM_HEADS, z, -_softplus(-z))
        scan = idx >= MLSTM_HEADS
    else:
        val = -_softplus(-(y + p2))
        scan = idx >= 0
    return val, scan


def _gates_kernel(x_ref, wg_ref, wgt_ref, p1_ref, p2_ref, p1t_ref, p2t_ref, row_ref, *rest, kind, seg):
    full = kind == "fox"
    tg = x_ref.shape[0]
    x = x_ref[...]
    yrow = lax.dot_general(wgt_ref[...], x, (((1,), (1,)), ((), ())),
                           preferred_element_type=F32)
    vrow, srow = _gate_act(kind, yrow, p1t_ref[...], p2t_ref[...], _iota2(yrow.shape, 0))
    crow = _scan_lanes(vrow, tg if full else seg)
    if full:
        (carry_ref,) = rest

        @pl.when(pl.program_id(1) == 0)
        def _():
            carry_ref[...] = jnp.zeros_like(carry_ref)
        crow = crow + carry_ref[...]
        carry_ref[...] = crow[:, tg - 1:tg]
        row_ref[...] = crow[:N_GATE_ROWS, :]
        return
    split_ref, _ = rest
    row_ref[...] = jnp.where(srow, crow, vrow)[:N_GATE_ROWS, :]
    ycol = jnp.dot(x, wg_ref[...], preferred_element_type=F32)
    vcol, scol = _gate_act(kind, ycol, p1_ref[...], p2_ref[...], _iota2(ycol.shape, 1))
    col = jnp.where(scol, _scan_rows(vcol, seg), vcol)
    hi = col.astype(BF16)
    r1 = col - hi.astype(F32)
    mid = r1.astype(BF16)
    lo = (r1 - mid.astype(F32)).astype(BF16)
    split_ref[:, 0:LANES] = hi
    split_ref[:, LANES:2 * LANES] = mid
    split_ref[:, 2 * LANES:3 * LANES] = lo


def _gates(xb, wg, p1, p2, kind, bsz, seq, tg):
    ng = wg.shape[1]
    wgp = jnp.pad(wg, ((0, 0), (0, LANES - ng))).astype(BF16)
    p1p = jnp.pad(p1, (0, LANES - p1.shape[0])).astype(F32)
    p2p = jnp.pad(p2, (0, LANES - p2.shape[0])).astype(F32)
    tg = min(tg, seq)
    nblk = seq // tg
    t = bsz * seq
    kern = functools.partial(_gates_kernel, kind=kind, seg=CHUNK)
    const = lambda b, s: (0, 0)
    row_shape = jax.ShapeDtypeStruct((N_GATE_ROWS, t), F32)
    row_spec = pl.BlockSpec((N_GATE_ROWS, tg), lambda b, s: (0, b * nblk + s))
    if kind == "fox":
        out_shape, out_specs = row_shape, row_spec
    else:
        out_shape = (row_shape, jax.ShapeDtypeStruct((t, N_SPLIT * LANES), BF16))
        out_specs = (row_spec, pl.BlockSpec((tg, N_SPLIT * LANES), lambda b, s: (b * nblk + s, 0)))
    return pl.pallas_call(
        kern,
        out_shape=out_shape,
        grid=(bsz, nblk),
        in_specs=[pl.BlockSpec((tg, D_MODEL), lambda b, s: (b * nblk + s, 0)),
                  pl.BlockSpec((D_MODEL, LANES), const),
                  pl.BlockSpec((LANES, D_MODEL), const),
                  pl.BlockSpec((1, LANES), const), pl.BlockSpec((1, LANES), const),
                  pl.BlockSpec((LANES, 1), const), pl.BlockSpec((LANES, 1), const)],
        out_specs=out_specs,
        scratch_shapes=[pltpu.VMEM((LANES, 1), F32)],
        compiler_params=_cparams("parallel", "arbitrary"),
        name=f"gates_{kind}")(xb, wgp, wgp.T, p1p[None, :], p2p[None, :], p1p[:, None], p2p[:, None])


def _gate_selectors(n_heads):
    r = jnp.arange(N_SPLIT * LANES)[None, :, None] % LANES
    c = jnp.arange(2 * LANES)[None, None, :]
    h = jnp.arange(n_heads)[:, None, None]
    return jnp.where(c < LANES, r == h, r == n_heads + h).astype(BF16)


def _memattn_kernel(x_ref, wq_ref, kv_ref, o_ref):
    q = jnp.dot(x_ref[...], wq_ref[...], preferred_element_type=F32).astype(BF16)
    scale = MEM_DH ** -0.5
    heads = range(MEM_HEADS)
    hc = [slice(h * MEM_DH, (h + 1) * MEM_DH) for h in heads]
    ones = jnp.ones((kv_ref.shape[1], LANES), BF16)
    logits = [_dot_nt(q[:, hc[h]], kv_ref[0, :, hc[h]]) * scale for h in heads]
    e = [jnp.exp(lg - jnp.max(lg, axis=-1, keepdims=True)) for lg in logits]
    pv = [_dot(e[h], jnp.concatenate([kv_ref[0, :, MEM_W + h * MEM_DH:MEM_W + (h + 1) * MEM_DH], ones], axis=1))
          for h in heads]
    for h in heads:
        o_ref[:, hc[h]] = (pv[h][:, :MEM_DH] / pv[h][:, MEM_DH:]).astype(o_ref.dtype)


def _mem_attention(xb, wq, kv, bsz, seq, tm):
    tm = min(tm, seq)
    nblk = seq // tm
    n_mem = kv.shape[1]
    return pl.pallas_call(
        _memattn_kernel,
        out_shape=jax.ShapeDtypeStruct((bsz * seq, MEM_W), BF16),
        grid=(bsz, nblk),
        in_specs=[pl.BlockSpec((tm, D_MODEL), lambda b, s: (b * nblk + s, 0)),
                  pl.BlockSpec((D_MODEL, MEM_W), lambda b, s: (0, 0)),
                  pl.BlockSpec((1, n_mem, 2 * MEM_W), lambda b, s: (b, 0, 0))],
        out_specs=pl.BlockSpec((tm, MEM_W), lambda b, s: (b * nblk + s, 0)),
        compiler_params=_cparams("parallel", "parallel"),
        name="mem_attention")(xb, wq, kv)


def _unit_lower_inverse(a_list):
    c = a_list[0].shape[0]
    ri, ci = _iota2((c, c), 0), _iota2((c, c), 1)
    eye = (ri == ci).astype(F32)
    shift = int(math.log2(INV_BLOCK))
    same_block = (ri >> shift) == (ci >> shift)
    d = [jnp.where(same_block, a, 0.0) for a in a_list]
    n = [a - di for a, di in zip(a_list, d)]
    t = [eye - di for di in d]
    p = d
    for _ in range(shift - 1):
        p = [_dot(pi, pi) for pi in p]
        t = [ti + _dot(ti, pi) for ti, pi in zip(t, p)]
    nb = c // INV_BLOCK
    if nb > 1:
        m = [_dot(ti, ni) for ti, ni in zip(t, n)]
        x = [eye - mi for mi in m]
        p = m
        for _ in range(int(math.log2(nb)) - 1):
            p = [_dot(pi, pi) for pi in p]
            x = [xi + _dot(xi, pi) for xi, pi in zip(x, p)]
        t = [_dot(xi, ti) for xi, ti in zip(x, t)]
    return t


def _gdn_kernel(q_ref, k_ref, v_ref, z_ref, gs_ref, grow_ref, sel_ref, ng_ref, o_ref,
                u_ref, w_ref, qg_ref, kg_ref, attn_ref, dl_ref, st_ref, *, chunk, group):
    ts = q_ref.shape[1]
    nchunk = ts // chunk
    s_idx = pl.program_id(1)
    blk = pl.program_id(0) * pl.num_programs(1) + s_idx

    @pl.when(s_idx == 0)
    def _():
        st_ref[...] = jnp.zeros_like(st_ref)

    ri, ci = _iota2((chunk, chunk), 0), _iota2((chunk, chunk), 1)
    incl = ri >= ci
    strict = ri > ci

    def intra(gi, carry):
        items = [(cc, h) for cc in range(group) for h in range(GDN_HEADS)]
        cidx = [gi * group + cc for cc in range(group)]
        rows = [pl.ds(pl.multiple_of(c * chunk, chunk), chunk) for c in cidx]
        gsp = [gs_ref[r, :] for r in rows]
        gb = [_dot(gsp[cc], sel_ref[h]) for cc, h in items]
        gcc = [g[:, :LANES] for g in gb]
        bc = [g[:, LANES:] for g in gb]
        q = [q_ref[0, rows[cc], h * GDN_D:(h + 1) * GDN_D] for cc, h in items]
        k = [k_ref[0, rows[cc], h * GDN_D:(h + 1) * GDN_D] for cc, h in items]
        v = [v_ref[0, rows[cc], h * GDN_D:(h + 1) * GDN_D] for cc, h in items]
        decay = [jnp.exp(jnp.where(incl, gcc[n][:, :chunk]
                                   - grow_ref[h, pl.ds(blk * nchunk + cidx[cc], 1), :], NEG_INF))
                 for n, (cc, h) in enumerate(items)]
        kb = [ki * bi for ki, bi in zip(k, bc)]
        kq = [_dot_nt(jnp.concatenate([kbi, qi], axis=0), ki) for kbi, qi, ki in zip(kb, q, k)]
        a = [jnp.where(strict, kqi[:chunk] * di, 0.0) for kqi, di in zip(kq, decay)]
        tinv = _unit_lower_inverse(a)
        egc = [jnp.exp(g) for g in gcc]
        sol = [_dot(ti, jnp.concatenate([vi * bi, kbi * ei], axis=1))
               for ti, vi, bi, kbi, ei in zip(tinv, v, bc, kb, egc)]
        for n, (cc, h) in enumerate(items):
            hc = slice(h * GDN_D, (h + 1) * GDN_D)
            glast = gcc[n][chunk - 1:chunk, :]
            u_ref[rows[cc], hc] = sol[n][:, :GDN_D]
            w_ref[rows[cc], hc] = sol[n][:, GDN_D:].astype(BF16)
            qg_ref[rows[cc], hc] = (q[n] * egc[n]).astype(BF16)
            kg_ref[rows[cc], hc] = (k[n] * jnp.exp(glast - gcc[n])).astype(BF16)
            attn_ref[h, rows[cc], :] = (kq[n][chunk:] * decay[n]).astype(BF16)
            dl_ref[h, pl.ds(cidx[cc], 1), :] = jnp.exp(glast)
        return carry

    lax.fori_loop(0, nchunk // group, intra, 0)

    def inter(c, carry):
        rows = pl.ds(pl.multiple_of(c * chunk, chunk), chunk)
        heads = range(GDN_HEADS)
        hcs = [slice(h * GDN_D, (h + 1) * GDN_D) for h in heads]
        ng = ng_ref[...]
        state = [st_ref[h] for h in heads]
        wq = [_dot(jnp.concatenate([w_ref[rows, hcs[h]], qg_ref[rows, hcs[h]]], axis=0), state[h])
              for h in heads]
        v_new = [(u_ref[rows, hcs[h]] - wq[h][:chunk]).astype(BF16) for h in heads]
        o2 = [_dot(attn_ref[h, rows, :], v_new[h]) for h in heads]
        upd = [_dot_tn(kg_ref[rows, hcs[h]], v_new[h]) for h in heads]
        for h in heads:
            st_ref[h] = state[h] * dl_ref[h, pl.ds(c, 1), :] + upd[h]
        for h in heads:
            o = wq[h][chunk:] + o2[h]
            z = z_ref[0, rows, hcs[h]].astype(F32)
            on = o * lax.rsqrt(jnp.mean(o * o, axis=-1, keepdims=True) + NORM_EPS) * ng
            o_ref[0, rows, hcs[h]] = (on * (z * _sigmoid(z))).astype(o_ref.dtype)
        return carry

    lax.fori_loop(0, nchunk, inter, 0)


def _gdn_mixer(q, k, v, zproj, gsplit, grow, norm_g, bsz, seq, ts, group):
    ts = min(ts, seq)
    nblk = seq // ts
    t = bsz * seq
    nchunk = ts // CHUNK
    kern = functools.partial(_gdn_kernel, chunk=CHUNK, group=min(group, nchunk))
    grow3 = grow.reshape(grow.shape[0], t // CHUNK, CHUNK)
    sel = _gate_selectors(GDN_HEADS)
    full = lambda a: pl.BlockSpec(a.shape, lambda b, s: (0,) * a.ndim, pipeline_mode=pl.Buffered(1))
    blk = pl.BlockSpec((1, ts, D_MODEL), lambda b, s: (b, s, 0))
    seq3 = lambda a: a.reshape(bsz, seq, D_MODEL)
    return pl.pallas_call(
        kern,
        out_shape=jax.ShapeDtypeStruct((bsz, seq, D_MODEL), BF16),
        grid=(bsz, nblk),
        in_specs=[blk, blk, blk, blk,
                  pl.BlockSpec((ts, N_SPLIT * LANES), lambda b, s: (b * nblk + s, 0)),
                  full(grow3), full(sel),
                  pl.BlockSpec((1, GDN_D), lambda b, s: (0, 0))],
        out_specs=blk,
        scratch_shapes=[pltpu.VMEM((ts, D_MODEL), F32),
                        pltpu.VMEM((ts, D_MODEL), BF16),
                        pltpu.VMEM((ts, D_MODEL), BF16),
                        pltpu.VMEM((ts, D_MODEL), BF16),
                        pltpu.VMEM((GDN_HEADS, ts, CHUNK), BF16),
                        pltpu.VMEM((GDN_HEADS, nchunk, LANES), F32),
                        pltpu.VMEM((GDN_HEADS, GDN_D, GDN_D), F32)],
        compiler_params=_cparams("parallel", "arbitrary"),
        name="gdn_mixer")(seq3(q), seq3(k), seq3(v), seq3(zproj), gsplit, grow3, sel, norm_g[None, :])


def _mlstm_kernel(x_ref, gs_ref, grow_ref, sel_ref, ng_ref, o_ref, st_ref, m_ref, *, chunk):
    ts = x_ref.shape[1]
    nchunk = ts // chunk
    s_idx = pl.program_id(1)
    blk = pl.program_id(0) * pl.num_programs(1) + s_idx
    nh, dv = MLSTM_HEADS, MLSTM_DV
    k0, v0, og0 = nh * MLSTM_DQK, 2 * nh * MLSTM_DQK, 2 * nh * MLSTM_DQK + nh * dv

    @pl.when(s_idx == 0)
    def _():
        st_ref[...] = jnp.zeros_like(st_ref)
        m_ref[...] = jnp.zeros_like(m_ref)

    incl = _iota2((chunk, chunk), 0) >= _iota2((chunk, chunk), 1)
    lo = _iota2((chunk, LANES), 1) < MLSTM_DQK
    ones = jnp.ones((chunk, LANES), x_ref.dtype)
    twice = lambda r: jnp.concatenate([r, r], axis=1)

    def body(c, carry):
        rows = pl.ds(pl.multiple_of(c * chunk, chunk), chunk)
        cg = blk * nchunk + c
        heads = range(nh)
        gsp = gs_ref[rows, :]
        gb = [_dot(gsp, sel_ref[h]) for h in heads]
        irep = [g[:, :LANES] for g in gb]
        brep = [g[:, LANES:] for g in gb]
        qscale = jnp.asarray(MLSTM_DQK ** -0.5, x_ref.dtype)
        q2 = [x_ref[0, rows, p * LANES:(p + 1) * LANES] * qscale for p in range(nh // 2)]
        k2 = [x_ref[0, rows, k0 + p * LANES:k0 + (p + 1) * LANES] for p in range(nh // 2)]
        zero = jnp.zeros_like(q2[0])
        q = [jnp.where(lo if h % 2 == 0 else ~lo, q2[h // 2], zero) for h in heads]
        k = [jnp.where(lo if h % 2 == 0 else ~lo, k2[h // 2], zero) for h in heads]
        vext = [jnp.concatenate([x_ref[0, rows, v0 + h * dv:v0 + (h + 1) * dv], ones], axis=1)
                for h in heads]
        qk = [_dot_nt(q[h], k2[h // 2]) for h in heads]
        state = [st_ref[h] for h in heads]
        qs = [_dot(q[h], state[h]) for h in heads]
        logd = [jnp.where(incl, brep[h][:, :chunk] - grow_ref[nh + h, pl.ds(cg, 1), :]
                          + grow_ref[h, pl.ds(cg, 1), :], NEG_INF) for h in heads]
        m_intra = [jnp.max(ld, axis=-1, keepdims=True) for ld in logd]
        blast = [b[chunk - 1:chunk, :] for b in brep]
        lw = [blast[h] - brep[h] + irep[h] for h in heads]
        mw = [jnp.max(x, axis=0, keepdims=True) for x in lw]
        m_prev = [m_ref[h:h + 1, :] for h in heads]
        a_inter = [brep[h] + m_prev[h] for h in heads]
        m_t = [jnp.maximum(a_inter[h], m_intra[h]) for h in heads]
        p = [qk[h] * jnp.exp(logd[h] - m_t[h][:, :chunk]) for h in heads]
        pv = [_dot(p[h], vext[h]) for h in heads]
        m_new = [jnp.maximum(blast[h] + m_prev[h], mw[h]) for h in heads]
        for h in heads:
            kw = k[h] * jnp.exp(lw[h] - m_new[h])
            keep = jnp.exp(blast[h] + m_prev[h] - m_new[h])
            st_ref[h] = twice(keep) * state[h] + _dot_tn(kw, vext[h])
            m_ref[h:h + 1, :] = m_new[h]
        for h in heads:
            num = twice(jnp.exp(a_inter[h] - m_t[h])) * qs[h] + pv[h]
            hh = num[:, :dv] / jnp.maximum(jnp.abs(num[:, dv:]), jnp.exp(-m_t[h]))
            mu = jnp.mean(hh, axis=-1, keepdims=True)
            xc = hh - mu
            var = jnp.mean(xc * xc, axis=-1, keepdims=True)
            hn = xc * lax.rsqrt(var + NORM_EPS) * ng_ref[h:h + 1, :]
            og = x_ref[0, rows, og0 + h * dv:og0 + (h + 1) * dv].astype(F32)
            o_ref[0, rows, h * dv:(h + 1) * dv] = (hn * _sigmoid(og)).astype(o_ref.dtype)
        return carry

    lax.fori_loop(0, nchunk, body, 0)


def _mlstm_mixer(proj, gsplit, grow, norm_g, bsz, seq, ts):
    ts = min(ts, seq)
    nblk = seq // ts
    t = bsz * seq
    ncols = proj.shape[1]
    kern = functools.partial(_mlstm_kernel, chunk=CHUNK)
    grow3 = grow.reshape(grow.shape[0], t // CHUNK, CHUNK)
    sel = _gate_selectors(MLSTM_HEADS)
    full = lambda a: pl.BlockSpec(a.shape, lambda b, s: (0,) * a.ndim)
    return pl.pallas_call(
        kern,
        out_shape=jax.ShapeDtypeStruct((bsz, seq, D_MODEL), BF16),
        grid=(bsz, nblk),
        in_specs=[pl.BlockSpec((1, ts, ncols), lambda b, s: (b, s, 0)),
                  pl.BlockSpec((ts, N_SPLIT * LANES), lambda b, s: (b * nblk + s, 0)),
                  full(grow3), full(sel),
                  pl.BlockSpec((MLSTM_HEADS, MLSTM_DV), lambda b, s: (0, 0))],
        out_specs=pl.BlockSpec((1, ts, D_MODEL), lambda b, s: (b, s, 0)),
        scratch_shapes=[pltpu.VMEM((MLSTM_HEADS, LANES, 2 * LANES), F32),
                        pltpu.VMEM((MLSTM_HEADS, LANES), F32)],
        compiler_params=_cparams("parallel", "arbitrary"),
        name="mlstm_mixer")(proj.reshape(bsz, seq, ncols), gsplit, grow3, sel, norm_g)


def _fox_kernel(q_ref, k_ref, v_ref, crow_ref, og_ref, theta_ref, o_ref, m_ref, acc_ref, *, tq, wide):
    i = pl.program_id(2)
    hp = pl.program_id(1)
    pair = range(2)
    seq = k_ref.shape[1]
    lo = _iota2((tq, LANES), 1) < FOX_DH
    q = q_ref[0]
    zero = jnp.zeros_like(q)
    qs = (jnp.where(lo, q, zero), jnp.where(lo, zero, q))
    q0 = pl.multiple_of(i * tq, tq)
    cref = [crow_ref[2 * hp + e, :, pl.ds(q0, LANES)][:, 0:1] for e in pair]
    pos = _iota2((1, seq), 1)
    first_live = [jnp.min(jnp.where(cref[e] - crow_ref[2 * hp + e] < -theta_ref[...], seq, pos)) for e in pair]
    n_dead = jnp.minimum(first_live[0], first_live[1])
    m_ref[...] = jnp.full(m_ref.shape, NEG_INF, F32)
    acc_ref[...] = jnp.zeros_like(acc_ref)
    causal = _iota2((tq, tq), 0) >= _iota2((tq, tq), 1)

    def step(k0, width, masked):
        k = k_ref[0, pl.ds(k0, width), :]
        vext = jnp.concatenate([v_ref[0, pl.ds(k0, width), :], jnp.ones((width, LANES), BF16)], axis=1)
        s = [lax.dot_general(qs[e], k, (((1,), (1,)), ((), ())), preferred_element_type=F32) for e in pair]
        for e in pair:
            bias = (cref[e] - crow_ref[2 * hp + e, :, pl.ds(k0, width)]) * LOG2E
            s[e] = s[e] + bias
            if masked:
                s[e] = jnp.where(causal, s[e], NEG_INF)
        m_prev = [m_ref[e] for e in pair]
        m_new = [jnp.maximum(m_prev[e], jnp.max(s[e], axis=-1, keepdims=True)) for e in pair]
        p = [jnp.exp2(s[e] - jnp.concatenate([m_new[e]] * (width // LANES), axis=1)) for e in pair]
        pv = [jnp.dot(p[e].astype(BF16), vext, preferred_element_type=F32) for e in pair]
        for e in pair:
            alpha = jnp.exp2(m_prev[e] - m_new[e])
            acc_ref[e] = jnp.concatenate([alpha, alpha], axis=1) * acc_ref[e] + pv[e]
            m_ref[e] = m_new[e]

    def body(j, carry):
        step(pl.multiple_of(j * (wide * tq), wide * tq), wide * tq, False)
        return carry

    lax.fori_loop(n_dead // (wide * tq), i // wide, body, 0)
    if wide == 2:
        @pl.when(jnp.logical_and(i % 2 == 1, q0 > n_dead))
        def _():
            step(pl.multiple_of((i - 1) * tq, tq), tq, False)
    step(q0, tq, True)
    out = jnp.where(lo, acc_ref[0, :, :LANES] / acc_ref[0, :, LANES:], acc_ref[1, :, :LANES] / acc_ref[1, :, LANES:])
    o_ref[0] = (out * _sigmoid(og_ref[0].astype(F32))).astype(o_ref.dtype)


def _fox_attention(qn, kn, vog, crow, qk_g, bsz, seq, tq, wide):
    tq = min(tq, seq)
    nq = seq // tq
    npair = FOX_HEADS // 2
    kern = functools.partial(_fox_kernel, tq=tq, wide=wide)
    og_blk = D_MODEL // LANES
    vog = vog.reshape(bsz, seq, 2 * D_MODEL)
    qk_bound = 1.02 * FOX_DH ** 0.5 * LOG2E * jnp.max(jnp.abs(qk_g[0])) * jnp.max(jnp.abs(qk_g[1]))
    theta = ((FOX_UNDERFLOW + 2.0 * qk_bound) / LOG2E).reshape(1, 1).astype(F32)
    return pl.pallas_call(
        kern,
        out_shape=jax.ShapeDtypeStruct((bsz, seq, D_MODEL), BF16),
        grid=(bsz, npair, nq),
        in_specs=[pl.BlockSpec((1, tq, LANES), lambda b, h, i: (b, i, h)),
                  pl.BlockSpec((1, seq, LANES), lambda b, h, i: (b, 0, h)),
                  pl.BlockSpec((1, seq, LANES), lambda b, h, i: (b, 0, h)),
                  pl.BlockSpec((FOX_HEADS, 1, seq), lambda b, h, i: (0, 0, b)),
                  pl.BlockSpec((1, tq, LANES), lambda b, h, i: (b, i, og_blk + h)),
                  pl.BlockSpec((1, 1), lambda b, h, i: (0, 0))],
        out_specs=pl.BlockSpec((1, tq, LANES), lambda b, h, i: (b, i, h)),
        scratch_shapes=[pltpu.VMEM((2, tq, LANES), F32), pltpu.VMEM((2, tq, 2 * LANES), F32)],
        compiler_params=_cparams("parallel", "parallel", "arbitrary"),
        name="fox_attention")(qn.reshape(bsz, seq, D_MODEL), kn.reshape(bsz, seq, D_MODEL),
                              vog, crow.reshape(FOX_HEADS, 1, bsz * seq), vog, theta)


def _layer_norm(z, g, b):
    mu = jnp.mean(z, axis=-1, keepdims=True)
    zc = z - mu
    var = jnp.mean(zc * zc, axis=-1, keepdims=True)
    return zc * lax.rsqrt(var + LN_EPS) * g + b


def _block_tail_kernel(mix_ref, mem_ref, wo_ref, x_ref, g1_ref, b1_ref, wu_ref, wd_ref, g2_ref, b2_ref,
                       o_ref, ob_ref, *, fchunk):
    y = jnp.dot(mix_ref[...], wo_ref[0:D_MODEL, :], preferred_element_type=F32)
    y = y + jnp.dot(mem_ref[...], wo_ref[D_MODEL:D_MODEL + MEM_W, :], preferred_element_type=F32)
    x1 = _layer_norm(ALPHA * x_ref[...] + y, g1_ref[...], b1_ref[...])
    xb = x1.astype(BF16)
    y = jnp.zeros(x_ref.shape, F32)
    for c in range(D_FF // fchunk):
        h = jnp.dot(xb, wu_ref[:, c * fchunk:(c + 1) * fchunk], preferred_element_type=F32)
        h = jnp.square(jnp.maximum(h, 0.0)).astype(BF16)
        y = y + jnp.dot(h, wd_ref[c * fchunk:(c + 1) * fchunk, :], preferred_element_type=F32)
    out = _layer_norm(ALPHA * x1 + y, g2_ref[...], b2_ref[...])
    o_ref[...] = out
    ob_ref[...] = out.astype(ob_ref.dtype)


def _block_tail(mix, memo, wo, x, g1, b1, wu, wd, g2, b2, tm, fchunk):
    t = x.shape[0]
    tm = min(tm, t)
    row = lambda n: pl.BlockSpec((tm, n), lambda i: (i, 0))
    const = lambda s: pl.BlockSpec(s, lambda i: (0, 0), pipeline_mode=pl.Buffered(1))
    vec = const((1, D_MODEL))
    kern = functools.partial(_block_tail_kernel, fchunk=fchunk)
    return pl.pallas_call(
        kern,
        out_shape=(jax.ShapeDtypeStruct((t, D_MODEL), F32), jax.ShapeDtypeStruct((t, D_MODEL), BF16)),
        grid=(t // tm,),
        in_specs=[row(D_MODEL), row(MEM_W), const((D_MODEL + MEM_W, D_MODEL)), row(D_MODEL), vec, vec,
                  const((D_MODEL, D_FF)), const((D_FF, D_MODEL)), vec, vec],
        out_specs=(row(D_MODEL), row(D_MODEL)),
        compiler_params=_cparams("parallel"),
        name="block_tail")(mix, memo, wo, x, g1[None, :], b1[None, :], wu, wd, g2[None, :], b2[None, :])


def kernel(x, mem, gdn_w_in, gdn_conv_w, gdn_a_log, gdn_dt_bias, gdn_norm_g, mlstm_w_in, mlstm_b_gate,
           mlstm_norm_g, fox_w_in, fox_b_f, fox_qk_g, mem_w_kv, w_out, ln1_g, ln1_b, w_up, w_down,
           ln2_g, ln2_b):
    bsz, seq, _ = x.shape
    t = bsz * seq
    n_mem = mem.shape[1]
    x = x.reshape(t, D_MODEL)
    xb = x.astype(BF16)
    memb = mem.reshape(bsz * n_mem, D_MODEL).astype(BF16)
    for i in range(DEPTH):
        kind, j = i % N_MIXERS, i // N_MIXERS
        if kind == 0:
            w_in, n_main, n_gate = gdn_w_in[j], 4 * D_MODEL, 2 * GDN_HEADS
        elif kind == 1:
            w_in, n_main, n_gate = mlstm_w_in[j], 3 * D_MODEL, 2 * MLSTM_HEADS
        else:
            w_in, n_main, n_gate = fox_w_in[j], 4 * D_MODEL, FOX_HEADS
        w_gate = w_in[:, n_main:n_main + n_gate]
        w_memq = w_in[:, n_main + n_gate:].astype(BF16)
        proj_in = lambda lo, hi, dt: _matmul(xb, w_in[:, lo:hi].astype(BF16), dt, TM_PROJ, TN_PROJ, "in_proj")
        kv = _matmul(memb, mem_w_kv[i].astype(BF16), BF16, TM_PROJ, TN_PROJ, "mem_kv")
        memo = _mem_attention(xb, w_memq, kv.reshape(bsz, n_mem, 2 * MEM_W), bsz, seq, TM_SMALL)
        wo = w_out[i]
        if kind == 0:
            grow, gsplit = _gates(xb, w_gate, gdn_a_log[j], gdn_dt_bias[j], "gdn", bsz, seq, TM_SMALL)
            qkv = [_proj_conv(xb, w_in[:, n * D_MODEL:(n + 1) * D_MODEL].astype(BF16),
                              gdn_conv_w[j][:, n * D_MODEL:(n + 1) * D_MODEL], seq, TM_PROJ, n < 2,
                              GDN_D ** -0.5 if n == 0 else 1.0) for n in range(3)]
            mix = _gdn_mixer(*qkv, proj_in(GDN_QKV, n_main, BF16), gsplit, grow, gdn_norm_g[j], bsz, seq,
                             TS_MIXER, GDN_GROUP)
        elif kind == 1:
            bias = jnp.concatenate([mlstm_b_gate[j, 0], mlstm_b_gate[j, 1]])
            grow, gsplit = _gates(xb, w_gate, bias, bias, "mlstm", bsz, seq, TM_SMALL)
            mix = _mlstm_mixer(proj_in(0, n_main, BF16), gsplit, grow, mlstm_norm_g[j], bsz, seq, TS_MIXER)
        else:
            order = jnp.argsort(fox_b_f[j])
            cols = (order[:, None] * FOX_DH + jnp.arange(FOX_DH)[None, :]).reshape(-1)
            part = lambda n: w_in[:, n * D_MODEL + cols].astype(BF16)
            crow = _gates(xb, w_gate[:, order], fox_b_f[j][order], fox_b_f[j][order], "fox", bsz, seq, TM_SMALL)
            qn = _proj_headnorm(xb, part(0), jnp.tile(fox_qk_g[j, 0], FOX_HEADS) * (FOX_DH ** -0.5 * LOG2E),
                                TM_PROJ)
            kn = _proj_headnorm(xb, part(1), jnp.tile(fox_qk_g[j, 1], FOX_HEADS), TM_PROJ)
            vog = _matmul(xb, jnp.concatenate([part(2), part(3)], axis=1), BF16, TM_PROJ, TN_PROJ, "in_proj")
            mix = _fox_attention(qn, kn, vog, crow, fox_qk_g[j], bsz, seq, TQ_FOX, FOX_WIDE)
            wo = jnp.concatenate([wo[:D_MODEL][cols], wo[D_MODEL:]], axis=0)
        x, xb = _block_tail(mix.reshape(t, D_MODEL), memo, wo.astype(BF16), x, ln1_g[i], ln1_b[i],
                            w_up[i].astype(BF16), w_down[i].astype(BF16), ln2_g[i], ln2_b[i], TM_TAIL, F_CHUNK)
    return x.reshape(bsz, seq, D_MODEL)
```

```python
import functools
import math

import jax
import jax.numpy as jnp
from jax import lax
from jax.experimental import pallas as pl
from jax.experimental.pallas import tpu as pltpu

F32 = jnp.float32
BF16 = jnp.bfloat16

D_MODEL = 1024
DEPTH = 4
N_MIXERS = 3
GDN_HEADS = 8
GDN_D = 128
GDN_QKV = 3 * GDN_HEADS * GDN_D
CONV_K = 4
MLSTM_HEADS = 8
MLSTM_DQK = 64
MLSTM_DV = 128
FOX_HEADS = 16
FOX_DH = 64
MEM_HEADS = 4
MEM_DH = 128
MEM_W = MEM_HEADS * MEM_DH
D_FF = 4 * D_MODEL
ALPHA = (2 * DEPTH) ** 0.25
LN_EPS = 1e-5
NORM_EPS = 1e-6

LANES = 128
SUBLANES = 8
VMEM_LIMIT = 56 * 1024 * 1024

TM_PROJ = 1024
TN_PROJ = 1024
TM_SMALL = 1024
TS_MIXER = 512
GDN_GROUP = 4
TQ_FOX = 512
FOX_WIDE = 2
TM_TAIL = 512
F_CHUNK = 1024

CHUNK = 64
N_GATE_ROWS = 16
N_SPLIT = 3
INV_BLOCK = 16
NEG_INF = float("-inf")
LOG2E = math.log2(math.e)
FOX_UNDERFLOW = 160.0


def _cparams(*sem):
    return pltpu.CompilerParams(dimension_semantics=sem, vmem_limit_bytes=VMEM_LIMIT)


def _dot(a, b):
    return jnp.dot(a.astype(BF16), b.astype(BF16), preferred_element_type=F32)


def _dot_nt(a, b):
    return lax.dot_general(a.astype(BF16), b.astype(BF16), (((1,), (1,)), ((), ())),
                           preferred_element_type=F32)


def _dot_tn(a, b):
    return lax.dot_general(a.astype(BF16), b.astype(BF16), (((0,), (0,)), ((), ())),
                           preferred_element_type=F32)


def _sigmoid(x):
    return 1.0 / (1.0 + jnp.exp(-x))


def _softplus(x):
    return jnp.maximum(x, 0.0) + jnp.log1p(jnp.exp(-jnp.abs(x)))


def _iota2(shape, dim):
    return lax.broadcasted_iota(jnp.int32, shape, dim)


def _mm_kernel(x_ref, w_ref, o_ref):
    o_ref[...] = jnp.dot(x_ref[...], w_ref[...], preferred_element_type=F32).astype(o_ref.dtype)


def _matmul(x, w, out_dtype, tm, tn, name):
    m, k = x.shape
    n = w.shape[1]
    tm, tn = min(tm, m), min(tn, n)
    assert m % tm == 0 and n % tn == 0
    return pl.pallas_call(
        _mm_kernel,
        out_shape=jax.ShapeDtypeStruct((m, n), out_dtype),
        grid=(m // tm, n // tn),
        in_specs=[pl.BlockSpec((tm, k), lambda i, j: (i, 0)),
                  pl.BlockSpec((k, tn), lambda i, j: (0, j))],
        out_specs=pl.BlockSpec((tm, tn), lambda i, j: (i, j)),
        compiler_params=_cparams("parallel", "parallel"),
        name=name)(x, w)


PROJ_SUB = 256


def _proj_conv_kernel(x_ref, xh_ref, w_ref, cw_ref, o_ref, ext_ref, *, seq, l2norm, scale):
    tm, tn = o_ref.shape
    halo = SUBLANES
    starts_sequence = (pl.program_id(0) * tm) % seq == 0
    x = x_ref[...]
    xh = xh_ref[...]
    for c in range(tn // PROJ_SUB):
        cs = slice(c * PROJ_SUB, (c + 1) * PROJ_SUB)
        ext_ref[halo:halo + tm, cs] = jnp.dot(x, w_ref[:, cs], preferred_element_type=F32)
        hist = jnp.dot(xh, w_ref[:, cs], preferred_element_type=F32)
        ext_ref[0:halo, cs] = jnp.where(starts_sequence, 0.0, hist)
        for b in range(PROJ_SUB // LANES):
            cols = slice(c * PROJ_SUB + b * LANES, c * PROJ_SUB + (b + 1) * LANES)
            acc = cw_ref[CONV_K - 1:CONV_K, cols] * ext_ref[halo:halo + tm, cols]
            for tap in range(CONV_K - 1):
                off = halo - (CONV_K - 1) + tap
                acc = acc + cw_ref[tap:tap + 1, cols] * ext_ref[off:off + tm, cols]
            y = acc * _sigmoid(acc)
            if l2norm:
                y = y * (lax.rsqrt(jnp.sum(y * y, axis=-1, keepdims=True) + NORM_EPS) * scale)
            o_ref[:, cols] = y


def _proj_conv(xb, w, cw, seq, tm, l2norm, scale):
    t, kdim = xb.shape
    n = w.shape[1]
    tm = min(tm, seq)
    assert seq % tm == 0 and n % PROJ_SUB == 0
    assert l2norm or scale == 1.0
    kern = functools.partial(_proj_conv_kernel, seq=seq, l2norm=l2norm, scale=scale)
    const = lambda s: pl.BlockSpec(s, lambda i: (0, 0))
    return pl.pallas_call(
        kern,
        out_shape=jax.ShapeDtypeStruct((t, n), F32),
        grid=(t // tm,),
        in_specs=[pl.BlockSpec((tm, kdim), lambda i: (i, 0)),
                  pl.BlockSpec((SUBLANES, kdim), lambda i: (jnp.maximum(i * (tm // SUBLANES) - 1, 0), 0)),
                  const((kdim, n)), const((CONV_K, n))],
        out_specs=pl.BlockSpec((tm, n), lambda i: (i, 0)),
        scratch_shapes=[pltpu.VMEM((tm + SUBLANES, n), F32)],
        compiler_params=_cparams("parallel"),
        name="proj_conv")(xb, xb, w, cw)


def _proj_headnorm_kernel(x_ref, w_ref, g_ref, o_ref):
    tm, tn = o_ref.shape
    half = _iota2((tm, LANES), 1) < FOX_DH
    x = x_ref[...]
    for c in range(tn // PROJ_SUB):
        raw = jnp.dot(x, w_ref[:, c * PROJ_SUB:(c + 1) * PROJ_SUB], preferred_element_type=F32)
        for b in range(PROJ_SUB // LANES):
            cols = slice(c * PROJ_SUB + b * LANES, c * PROJ_SUB + (b + 1) * LANES)
            y = raw[:, b * LANES:(b + 1) * LANES]
            sq = y * y
            s_lo = jnp.sum(jnp.where(half, sq, 0.0), axis=-1, keepdims=True)
            s_all = jnp.sum(sq, axis=-1, keepdims=True)
            ms = jnp.where(half, s_lo, s_all - s_lo) * (1.0 / FOX_DH)
            o_ref[:, cols] = (y * lax.rsqrt(ms + NORM_EPS) * g_ref[:, cols]).astype(o_ref.dtype)


def _proj_headnorm(xb, w, gain, tm):
    t, kdim = xb.shape
    n = w.shape[1]
    tm = min(tm, t)
    const = lambda s: pl.BlockSpec(s, lambda i: (0, 0))
    return pl.pallas_call(
        _proj_headnorm_kernel,
        out_shape=jax.ShapeDtypeStruct((t, n), BF16),
        grid=(t // tm,),
        in_specs=[pl.BlockSpec((tm, kdim), lambda i: (i, 0)), const((kdim, n)), const((1, n))],
        out_specs=pl.BlockSpec((tm, n), lambda i: (i, 0)),
        compiler_params=_cparams("parallel"),
        name="proj_headnorm")(xb, w, gain[None, :].astype(F32))


def _scan_rows(x, seg):
    pos = _iota2(x.shape, 0) & (seg - 1)
    d = 1
    while d < seg:
        x = x + jnp.where(pos >= d, pltpu.roll(x, d, 0), 0.0)
        d *= 2
    return x


def _scan_lanes(x, seg):
    pos = _iota2(x.shape, 1) & (seg - 1)
    d = 1
    while d < seg:
        x = x + jnp.where(pos >= d, pltpu.roll(x, d, 1), 0.0)
        d *= 2
    return x


def _gate_act(kind, y, p1, p2, idx):
    if kind == "gdn":
        g = -jnp.exp(p1) * _softplus(y + p2)
        val = jnp.where(idx < GDN_HEADS, g, _sigmoid(y))
        scan = idx < GDN_HEADS
    elif kind == "mlstm":
        z = y + p2
        val = jnp.where(idx < MLSTM_HEADS, z, -_softplus(-z))
        scan = idx >= MLSTM_HEADS
    else:
        val = -_softplus(-(y + p2))
        scan = idx >= 0
    return val, scan


def _gates_kernel(x_ref, wg_ref, wgt_ref, p1_ref, p2_ref, p1t_ref, p2t_ref, row_ref, *rest, kind, seg):
    full = kind == "fox"
    tg = x_ref.shape[0]
    x = x_ref[...]
    yrow = lax.dot_general(wgt_ref[...], x, (((1,), (1,)), ((), ())),
                           preferred_element_type=F32)
    vrow, srow = _gate_act(kind, yrow, p1t_ref[...], p2t_ref[...], _iota2(yrow.shape, 0))
    crow = _scan_lanes(vrow, tg if full else seg)
    if full:
        (carry_ref,) = rest

        @pl.when(pl.program_id(1) == 0)
        def _():
            carry_ref[...] = jnp.zeros_like(carry_ref)
        crow = crow + carry_ref[...]
        carry_ref[...] = crow[:, tg - 1:tg]
        row_ref[...] = crow[:N_GATE_ROWS, :]
        return
    split_ref, _ = rest
    row_ref[...] = jnp.where(srow, crow, vrow)[:N_GATE_ROWS, :]
    ycol = jnp.dot(x, wg_ref[...], preferred_element_type=F32)
    vcol, scol = _gate_act(kind, ycol, p1_ref[...], p2_ref[...], _iota2(ycol.shape, 1))
    col = jnp.where(scol, _scan_rows(vcol, seg), vcol)
    hi = col.astype(BF16).astype(F32)
    r1 = col - hi
    mid = r1.astype(BF16).astype(F32)
    lo = r1 - mid
    lane = _iota2(col.shape, 1)
    packed = jnp.where(lane < N_GATE_ROWS, hi,
                       jnp.where(lane < 2 * N_GATE_ROWS, pltpu.roll(mid, N_GATE_ROWS, 1),
                                 jnp.where(lane < 3 * N_GATE_ROWS, pltpu.roll(lo, 2 * N_GATE_ROWS, 1), 0.0)))
    split_ref[...] = packed.astype(BF16)


def _gates(xb, wg, p1, p2, kind, bsz, seq, tg):
    ng = wg.shape[1]
    wgp = jnp.pad(wg, ((0, 0), (0, LANES - ng))).astype(BF16)
    p1p = jnp.pad(p1, (0, LANES - p1.shape[0])).astype(F32)
    p2p = jnp.pad(p2, (0, LANES - p2.shape[0])).astype(F32)
    tg = min(tg, seq)
    nblk = seq // tg
    t = bsz * seq
    kern = functools.partial(_gates_kernel, kind=kind, seg=CHUNK)
    const = lambda b, s: (0, 0)
    row_shape = jax.ShapeDtypeStruct((N_GATE_ROWS, t), F32)
    row_spec = pl.BlockSpec((N_GATE_ROWS, tg), lambda b, s: (0, b * nblk + s))
    if kind == "fox":
        out_shape, out_specs = row_shape, row_spec
    else:
        out_shape = (row_shape, jax.ShapeDtypeStruct((t, LANES), BF16))
        out_specs = (row_spec, pl.BlockSpec((tg, LANES), lambda b, s: (b * nblk + s, 0)))
    return pl.pallas_call(
        kern,
        out_shape=out_shape,
        grid=(bsz, nblk),
        in_specs=[pl.BlockSpec((tg, D_MODEL), lambda b, s: (b * nblk + s, 0)),
                  pl.BlockSpec((D_MODEL, LANES), const),
                  pl.BlockSpec((LANES, D_MODEL), const),
                  pl.BlockSpec((1, LANES), const), pl.BlockSpec((1, LANES), const),
                  pl.BlockSpec((LANES, 1), const), pl.BlockSpec((LANES, 1), const)],
        out_specs=out_specs,
        scratch_shapes=[pltpu.VMEM((LANES, 1), F32)],
        compiler_params=_cparams("parallel", "arbitrary"),
        name=f"gates_{kind}")(xb, wgp, wgp.T, p1p[None, :], p2p[None, :], p1p[:, None], p2p[:, None])


def _gate_selectors(n_heads):
    r = jnp.arange(LANES)[None, :, None]
    c = jnp.arange(2 * LANES)[None, None, :]
    h = jnp.arange(n_heads)[:, None, None]
    piece = r < N_SPLIT * N_GATE_ROWS
    gate = r % N_GATE_ROWS
    return (piece & jnp.where(c < LANES, gate == h, gate == n_heads + h)).astype(BF16)


def _memattn_kernel(x_ref, wq_ref, kv_ref, o_ref):
    q = jnp.dot(x_ref[...], wq_ref[...], preferred_element_type=F32).astype(BF16)
    scale = MEM_DH ** -0.5
    heads = range(MEM_HEADS)
    hc = [slice(h * MEM_DH, (h + 1) * MEM_DH) for h in heads]
    ones = jnp.ones((kv_ref.shape[1], LANES), BF16)
    logits = [_dot_nt(q[:, hc[h]], kv_ref[0, :, hc[h]]) * scale for h in heads]
    e = [jnp.exp(lg - jnp.max(lg, axis=-1, keepdims=True)) for lg in logits]
    pv = [_dot(e[h], jnp.concatenate([kv_ref[0, :, MEM_W + h * MEM_DH:MEM_W + (h + 1) * MEM_DH], ones], axis=1))
          for h in heads]
    for h in heads:
        o_ref[:, hc[h]] = (pv[h][:, :MEM_DH] / pv[h][:, MEM_DH:]).astype(o_ref.dtype)


def _mem_attention(xb, wq, kv, bsz, seq, tm):
    tm = min(tm, seq)
    nblk = seq // tm
    n_mem = kv.shape[1]
    return pl.pallas_call(
        _memattn_kernel,
        out_shape=jax.ShapeDtypeStruct((bsz * seq, MEM_W), BF16),
        grid=(bsz, nblk),
        in_specs=[pl.BlockSpec((tm, D_MODEL), lambda b, s: (b * nblk + s, 0)),
                  pl.BlockSpec((D_MODEL, MEM_W), lambda b, s: (0, 0)),
                  pl.BlockSpec((1, n_mem, 2 * MEM_W), lambda b, s: (b, 0, 0))],
        out_specs=pl.BlockSpec((tm, MEM_W), lambda b, s: (b * nblk + s, 0)),
        compiler_params=_cparams("parallel", "parallel"),
        name="mem_attention")(xb, wq, kv)


def _unit_lower_inverse(a_list):
    c = a_list[0].shape[0]
    ri, ci = _iota2((c, c), 0), _iota2((c, c), 1)
    eye = (ri == ci).astype(F32)
    shift = int(math.log2(INV_BLOCK))
    same_block = (ri >> shift) == (ci >> shift)
    d = [jnp.where(same_block, a, 0.0) for a in a_list]
    n = [a - di for a, di in zip(a_list, d)]
    t = [eye - di for di in d]
    p = d
    for _ in range(shift - 1):
        p = [_dot(pi, pi) for pi in p]
        t = [ti + _dot(ti, pi) for ti, pi in zip(t, p)]
    nb = c // INV_BLOCK
    if nb > 1:
        m = [_dot(ti, ni) for ti, ni in zip(t, n)]
        x = [eye - mi for mi in m]
        p = m
        for _ in range(int(math.log2(nb)) - 1):
            p = [_dot(pi, pi) for pi in p]
            x = [xi + _dot(xi, pi) for xi, pi in zip(x, p)]
        t = [_dot(xi, ti) for xi, ti in zip(x, t)]
    return t


def _gdn_kernel(q_ref, k_ref, v_ref, z_ref, gs_ref, grow_ref, sel_ref, ng_ref, o_ref,
                u_ref, w_ref, qg_ref, kg_ref, attn_ref, dl_ref, st_ref, *, chunk, group):
    ts = q_ref.shape[1]
    nchunk = ts // chunk
    s_idx = pl.program_id(1)
    blk = pl.program_id(0) * pl.num_programs(1) + s_idx

    @pl.when(s_idx == 0)
    def _():
        st_ref[...] = jnp.zeros_like(st_ref)

    ri, ci = _iota2((chunk, chunk), 0), _iota2((chunk, chunk), 1)
    incl = ri >= ci
    strict = ri > ci

    def intra(gi, carry):
        items = [(cc, h) for cc in range(group) for h in range(GDN_HEADS)]
        cidx = [gi * group + cc for cc in range(group)]
        rows = [pl.ds(pl.multiple_of(c * chunk, chunk), chunk) for c in cidx]
        gsp = [gs_ref[r, :] for r in rows]
        gb = [_dot(gsp[cc], sel_ref[h]) for cc, h in items]
        gcc = [g[:, :LANES] for g in gb]
        bc = [g[:, LANES:] for g in gb]
        q = [q_ref[0, rows[cc], h * GDN_D:(h + 1) * GDN_D] for cc, h in items]
        k = [k_ref[0, rows[cc], h * GDN_D:(h + 1) * GDN_D] for cc, h in items]
        v = [v_ref[0, rows[cc], h * GDN_D:(h + 1) * GDN_D] for cc, h in items]
        decay = [jnp.exp(jnp.where(incl, gcc[n][:, :chunk]
                                   - grow_ref[h, pl.ds(blk * nchunk + cidx[cc], 1), :], NEG_INF))
                 for n, (cc, h) in enumerate(items)]
        kb = [ki * bi for ki, bi in zip(k, bc)]
        kq = [_dot_nt(jnp.concatenate([kbi, qi], axis=0), ki) for kbi, qi, ki in zip(kb, q, k)]
        a = [jnp.where(strict, kqi[:chunk] * di, 0.0) for kqi, di in zip(kq, decay)]
        tinv = _unit_lower_inverse(a)
        egc = [jnp.exp(g) for g in gcc]
        sol = [_dot(ti, jnp.concatenate([vi * bi, kbi * ei], axis=1))
               for ti, vi, bi, kbi, ei in zip(tinv, v, bc, kb, egc)]
        for n, (cc, h) in enumerate(items):
            hc = slice(h * GDN_D, (h + 1) * GDN_D)
            glast = gcc[n][chunk - 1:chunk, :]
            u_ref[rows[cc], hc] = sol[n][:, :GDN_D]
            w_ref[rows[cc], hc] = sol[n][:, GDN_D:].astype(BF16)
            qg_ref[rows[cc], hc] = (q[n] * egc[n]).astype(BF16)
            kg_ref[rows[cc], hc] = (k[n] * jnp.exp(glast - gcc[n])).astype(BF16)
            attn_ref[h, rows[cc], :] = (kq[n][chunk:] * decay[n]).astype(BF16)
            dl_ref[h, pl.ds(cidx[cc], 1), :] = jnp.exp(glast)
        return carry

    lax.fori_loop(0, nchunk // group, intra, 0)

    def inter(c, carry):
        rows = pl.ds(pl.multiple_of(c * chunk, chunk), chunk)
        heads = range(GDN_HEADS)
        hcs = [slice(h * GDN_D, (h + 1) * GDN_D) for h in heads]
        ng = ng_ref[...]
        state = [st_ref[h] for h in heads]
        wq = [_dot(jnp.concatenate([w_ref[rows, hcs[h]], qg_ref[rows, hcs[h]]], axis=0), state[h])
              for h in heads]
        v_new = [(u_ref[rows, hcs[h]] - wq[h][:chunk]).astype(BF16) for h in heads]
        o2 = [_dot(attn_ref[h, rows, :], v_new[h]) for h in heads]
        upd = [_dot_tn(kg_ref[rows, hcs[h]], v_new[h]) for h in heads]
        for h in heads:
            st_ref[h] = state[h] * dl_ref[h, pl.ds(c, 1), :] + upd[h]
        for h in heads:
            o = wq[h][chunk:] + o2[h]
            z = z_ref[0, rows, hcs[h]].astype(F32)
            on = o * lax.rsqrt(jnp.mean(o * o, axis=-1, keepdims=True) + NORM_EPS) * ng
            o_ref[0, rows, hcs[h]] = (on * (z * _sigmoid(z))).astype(o_ref.dtype)
        return carry

    lax.fori_loop(0, nchunk, inter, 0, unroll=2)


def _gdn_mixer(q, k, v, zproj, gsplit, grow, norm_g, bsz, seq, ts, group):
    ts = min(ts, seq)
    nblk = seq // ts
    t = bsz * seq
    nchunk = ts // CHUNK
    kern = functools.partial(_gdn_kernel, chunk=CHUNK, group=min(group, nchunk))
    grow3 = grow.reshape(grow.shape[0], t // CHUNK, CHUNK)
    sel = _gate_selectors(GDN_HEADS)
    full = lambda a: pl.BlockSpec(a.shape, lambda b, s: (0,) * a.ndim, pipeline_mode=pl.Buffered(1))
    blk = pl.BlockSpec((1, ts, D_MODEL), lambda b, s: (b, s, 0))
    seq3 = lambda a: a.reshape(bsz, seq, D_MODEL)
    return pl.pallas_call(
        kern,
        out_shape=jax.ShapeDtypeStruct((bsz, seq, D_MODEL), BF16),
        grid=(bsz, nblk),
        in_specs=[blk, blk, blk, blk,
                  pl.BlockSpec((ts, LANES), lambda b, s: (b * nblk + s, 0)),
                  full(grow3), full(sel),
                  pl.BlockSpec((1, GDN_D), lambda b, s: (0, 0))],
        out_specs=blk,
        scratch_shapes=[pltpu.VMEM((ts, D_MODEL), F32),
                        pltpu.VMEM((ts, D_MODEL), BF16),
                        pltpu.VMEM((ts, D_MODEL), BF16),
                        pltpu.VMEM((ts, D_MODEL), BF16),
                        pltpu.VMEM((GDN_HEADS, ts, CHUNK), BF16),
                        pltpu.VMEM((GDN_HEADS, nchunk, LANES), F32),
                        pltpu.VMEM((GDN_HEADS, GDN_D, GDN_D), F32)],
        compiler_params=_cparams("parallel", "arbitrary"),
        name="gdn_mixer")(seq3(q), seq3(k), seq3(v), seq3(zproj), gsplit, grow3, sel, norm_g[None, :])


def _mlstm_kernel(x_ref, gs_ref, grow_ref, sel_ref, ng_ref, o_ref, st_ref, m_ref, *, chunk):
    ts = x_ref.shape[1]
    nchunk = ts // chunk
    s_idx = pl.program_id(1)
    blk = pl.program_id(0) * pl.num_programs(1) + s_idx
    nh, dv = MLSTM_HEADS, MLSTM_DV
    k0, v0, og0 = nh * MLSTM_DQK, 2 * nh * MLSTM_DQK, 2 * nh * MLSTM_DQK + nh * dv

    @pl.when(s_idx == 0)
    def _():
        st_ref[...] = jnp.zeros_like(st_ref)
        m_ref[...] = jnp.zeros_like(m_ref)

    incl = _iota2((chunk, chunk), 0) >= _iota2((chunk, chunk), 1)
    lo = _iota2((chunk, LANES), 1) < MLSTM_DQK
    ones = jnp.ones((chunk, LANES), x_ref.dtype)
    twice = lambda r: jnp.concatenate([r, r], axis=1)

    def body(c, carry):
        rows = pl.ds(pl.multiple_of(c * chunk, chunk), chunk)
        cg = blk * nchunk + c
        heads = range(nh)
        gsp = gs_ref[rows, :]
        gb = [_dot(gsp, sel_ref[h]) for h in heads]
        irep = [g[:, :LANES] for g in gb]
        brep = [g[:, LANES:] for g in gb]
        qscale = jnp.asarray(MLSTM_DQK ** -0.5, x_ref.dtype)
        q2 = [x_ref[0, rows, p * LANES:(p + 1) * LANES] * qscale for p in range(nh // 2)]
        k2 = [x_ref[0, rows, k0 + p * LANES:k0 + (p + 1) * LANES] for p in range(nh // 2)]
        zero = jnp.zeros_like(q2[0])
        q = [jnp.where(lo if h % 2 == 0 else ~lo, q2[h // 2], zero) for h in heads]
        k = [jnp.where(lo if h % 2 == 0 else ~lo, k2[h // 2], zero) for h in heads]
        vext = [jnp.concatenate([x_ref[0, rows, v0 + h * dv:v0 + (h + 1) * dv], ones], axis=1)
                for h in heads]
        qk = [_dot_nt(q[h], k2[h // 2]) for h in heads]
        state = [st_ref[h] for h in heads]
        qs = [_dot(q[h], state[h]) for h in heads]
        logd = [jnp.where(incl, brep[h][:, :chunk] - grow_ref[nh + h, pl.ds(cg, 1), :]
                          + grow_ref[h, pl.ds(cg, 1), :], NEG_INF) for h in heads]
        m_intra = [jnp.max(ld, axis=-1, keepdims=True) for ld in logd]
        blast = [b[chunk - 1:chunk, :] for b in brep]
        lw = [blast[h] - brep[h] + irep[h] for h in heads]
        mw = [jnp.max(x, axis=0, keepdims=True) for x in lw]
        m_prev = [m_ref[h:h + 1, :] for h in heads]
        a_inter = [brep[h] + m_prev[h] for h in heads]
        m_t = [jnp.maximum(a_inter[h], m_intra[h]) for h in heads]
        p = [qk[h] * jnp.exp(logd[h] - m_t[h][:, :chunk]) for h in heads]
        pv = [_dot(p[h], vext[h]) for h in heads]
        m_new = [jnp.maximum(blast[h] + m_prev[h], mw[h]) for h in heads]
        for h in heads:
            kw = k[h] * jnp.exp(lw[h] - m_new[h])
            keep = jnp.exp(blast[h] + m_prev[h] - m_new[h])
            st_ref[h] = twice(keep) * state[h] + _dot_tn(kw, vext[h])
            m_ref[h:h + 1, :] = m_new[h]
        for h in heads:
            num = twice(jnp.exp(a_inter[h] - m_t[h])) * qs[h] + pv[h]
            hh = num[:, :dv] / jnp.maximum(jnp.abs(num[:, dv:]), jnp.exp(-m_t[h]))
            mu = jnp.mean(hh, axis=-1, keepdims=True)
            xc = hh - mu
            var = jnp.mean(xc * xc, axis=-1, keepdims=True)
            hn = xc * lax.rsqrt(var + NORM_EPS) * ng_ref[h:h + 1, :]
            og = x_ref[0, rows, og0 + h * dv:og0 + (h + 1) * dv].astype(F32)
            o_ref[0, rows, h * dv:(h + 1) * dv] = (hn * _sigmoid(og)).astype(o_ref.dtype)
        return carry

    lax.fori_loop(0, nchunk, body, 0)


def _mlstm_mixer(proj, gsplit, grow, norm_g, bsz, seq, ts):
    ts = min(ts, seq)
    nblk = seq // ts
    t = bsz * seq
    ncols = proj.shape[1]
    kern = functools.partial(_mlstm_kernel, chunk=CHUNK)
    grow3 = grow.reshape(grow.shape[0], t // CHUNK, CHUNK)
    sel = _gate_selectors(MLSTM_HEADS)
    full = lambda a: pl.BlockSpec(a.shape, lambda b, s: (0,) * a.ndim)
    return pl.pallas_call(
        kern,
        out_shape=jax.ShapeDtypeStruct((bsz, seq, D_MODEL), BF16),
        grid=(bsz, nblk),
        in_specs=[pl.BlockSpec((1, ts, ncols), lambda b, s: (b, s, 0)),
                  pl.BlockSpec((ts, LANES), lambda b, s: (b * nblk + s, 0)),
                  full(grow3), full(sel),
                  pl.BlockSpec((MLSTM_HEADS, MLSTM_DV), lambda b, s: (0, 0))],
        out_specs=pl.BlockSpec((1, ts, D_MODEL), lambda b, s: (b, s, 0)),
        scratch_shapes=[pltpu.VMEM((MLSTM_HEADS, LANES, 2 * LANES), F32),
                        pltpu.VMEM((MLSTM_HEADS, LANES), F32)],
        compiler_params=_cparams("parallel", "arbitrary"),
        name="mlstm_mixer")(proj.reshape(bsz, seq, ncols), gsplit, grow3, sel, norm_g)


def _fox_kernel(q_ref, k_ref, v_ref, crow_ref, og_ref, theta_ref, o_ref, m_ref, acc_ref, *, tq, wide):
    i = pl.program_id(2)
    hp = pl.program_id(1)
    pair = range(2)
    seq = k_ref.shape[1]
    lo = _iota2((tq, LANES), 1) < FOX_DH
    q = q_ref[0]
    zero = jnp.zeros_like(q)
    qs = (jnp.where(lo, q, zero), jnp.where(lo, zero, q))
    q0 = pl.multiple_of(i * tq, tq)
    cref = [crow_ref[2 * hp + e, :, pl.ds(q0, LANES)][:, 0:1] for e in pair]
    pos = _iota2((1, seq), 1)
    first_live = [jnp.min(jnp.where(cref[e] - crow_ref[2 * hp + e] < -theta_ref[...], seq, pos)) for e in pair]
    n_dead = jnp.minimum(first_live[0], first_live[1])
    m_ref[...] = jnp.full(m_ref.shape, NEG_INF, F32)
    acc_ref[...] = jnp.zeros_like(acc_ref)
    causal = _iota2((tq, tq), 0) >= _iota2((tq, tq), 1)

    def step(k0, width, masked):
        k = k_ref[0, pl.ds(k0, width), :]
        vext = jnp.concatenate([v_ref[0, pl.ds(k0, width), :], jnp.ones((width, LANES), BF16)], axis=1)
        s = [lax.dot_general(qs[e], k, (((1,), (1,)), ((), ())), preferred_element_type=F32) for e in pair]
        for e in pair:
            bias = (cref[e] - crow_ref[2 * hp + e, :, pl.ds(k0, width)]) * LOG2E
            s[e] = s[e] + bias
            if masked:
                s[e] = jnp.where(causal, s[e], NEG_INF)
        m_prev = [m_ref[e] for e in pair]
        m_new = [jnp.maximum(m_prev[e], jnp.max(s[e], axis=-1, keepdims=True)) for e in pair]
        p = [jnp.exp2(s[e] - jnp.concatenate([m_new[e]] * (width // LANES), axis=1)) for e in pair]
        pv = [jnp.dot(p[e].astype(BF16), vext, preferred_element_type=F32) for e in pair]
        for e in pair:
            alpha = jnp.exp2(m_prev[e] - m_new[e])
            acc_ref[e] = jnp.concatenate([alpha, alpha], axis=1) * acc_ref[e] + pv[e]
            m_ref[e] = m_new[e]

    def body(j, carry):
        step(pl.multiple_of(j * (wide * tq), wide * tq), wide * tq, False)
        return carry

    lax.fori_loop(n_dead // (wide * tq), i // wide, body, 0)
    if wide == 2:
        @pl.when(jnp.logical_and(i % 2 == 1, q0 > n_dead))
        def _():
            step(pl.multiple_of((i - 1) * tq, tq), tq, False)
    step(q0, tq, True)
    out = jnp.where(lo, acc_ref[0, :, :LANES] / acc_ref[0, :, LANES:], acc_ref[1, :, :LANES] / acc_ref[1, :, LANES:])
    o_ref[0] = (out * _sigmoid(og_ref[0].astype(F32))).astype(o_ref.dtype)


def _fox_attention(qn, kn, vog, crow, qk_g, bsz, seq, tq, wide):
    tq = min(tq, seq)
    nq = seq // tq
    npair = FOX_HEADS // 2
    kern = functools.partial(_fox_kernel, tq=tq, wide=wide)
    og_blk = D_MODEL // LANES
    vog = vog.reshape(bsz, seq, 2 * D_MODEL)
    qk_bound = 1.02 * FOX_DH ** 0.5 * LOG2E * jnp.max(jnp.abs(qk_g[0])) * jnp.max(jnp.abs(qk_g[1]))
    theta = ((FOX_UNDERFLOW + 2.0 * qk_bound) / LOG2E).reshape(1, 1).astype(F32)
    return pl.pallas_call(
        kern,
        out_shape=jax.ShapeDtypeStruct((bsz, seq, D_MODEL), BF16),
        grid=(bsz, npair, nq),
        in_specs=[pl.BlockSpec((1, tq, LANES), lambda b, h, i: (b, i, h)),
                  pl.BlockSpec((1, seq, LANES), lambda b, h, i: (b, 0, h)),
                  pl.BlockSpec((1, seq, LANES), lambda b, h, i: (b, 0, h)),
                  pl.BlockSpec((FOX_HEADS, 1, seq), lambda b, h, i: (0, 0, b)),
                  pl.BlockSpec((1, tq, LANES), lambda b, h, i: (b, i, og_blk + h)),
                  pl.BlockSpec((1, 1), lambda b, h, i: (0, 0))],
        out_specs=pl.BlockSpec((1, tq, LANES), lambda b, h, i: (b, i, h)),
        scratch_shapes=[pltpu.VMEM((2, tq, LANES), F32), pltpu.VMEM((2, tq, 2 * LANES), F32)],
        compiler_params=_cparams("parallel", "parallel", "arbitrary"),
        name="fox_attention")(qn.reshape(bsz, seq, D_MODEL), kn.reshape(bsz, seq, D_MODEL),
                              vog, crow.reshape(FOX_HEADS, 1, bsz * seq), vog, theta)


def _layer_norm(z, g, b):
    mu = jnp.mean(z, axis=-1, keepdims=True)
    zc = z - mu
    var = jnp.mean(zc * zc, axis=-1, keepdims=True)
    return zc * lax.rsqrt(var + LN_EPS) * g + b


def _block_tail_kernel(mix_ref, mem_ref, wo_ref, x_ref, g1_ref, b1_ref, wu_ref, wd_ref, g2_ref, b2_ref,
                       o_ref, ob_ref, *, fchunk):
    y = jnp.dot(mix_ref[...], wo_ref[0:D_MODEL, :], preferred_element_type=F32)
    y = y + jnp.dot(mem_ref[...], wo_ref[D_MODEL:D_MODEL + MEM_W, :], preferred_element_type=F32)
    x1 = _layer_norm(ALPHA * x_ref[...] + y, g1_ref[...], b1_ref[...])
    xb = x1.astype(BF16)
    y = jnp.zeros(x_ref.shape, F32)
    for c in range(D_FF // fchunk):
        h = jnp.dot(xb, wu_ref[:, c * fchunk:(c + 1) * fchunk], preferred_element_type=F32)
        h = jnp.square(jnp.maximum(h, 0.0)).astype(BF16)
        y = y + jnp.dot(h, wd_ref[c * fchunk:(c + 1) * fchunk, :], preferred_element_type=F32)
    out = _layer_norm(ALPHA * x1 + y, g2_ref[...], b2_ref[...])
    o_ref[...] = out
    ob_ref[...] = out.astype(ob_ref.dtype)


def _block_tail(mix, memo, wo, x, g1, b1, wu, wd, g2, b2, tm, fchunk):
    t = x.shape[0]
    tm = min(tm, t)
    row = lambda n: pl.BlockSpec((tm, n), lambda i: (i, 0))
    const = lambda s: pl.BlockSpec(s, lambda i: (0, 0), pipeline_mode=pl.Buffered(1))
    vec = const((1, D_MODEL))
    kern = functools.partial(_block_tail_kernel, fchunk=fchunk)
    return pl.pallas_call(
        kern,
        out_shape=(jax.ShapeDtypeStruct((t, D_MODEL), F32), jax.ShapeDtypeStruct((t, D_MODEL), BF16)),
        grid=(t // tm,),
        in_specs=[row(D_MODEL), row(MEM_W), const((D_MODEL + MEM_W, D_MODEL)), row(D_MODEL), vec, vec,
                  const((D_MODEL, D_FF)), const((D_FF, D_MODEL)), vec, vec],
        out_specs=(row(D_MODEL), row(D_MODEL)),
        compiler_params=_cparams("parallel"),
        name="block_tail")(mix, memo, wo, x, g1[None, :], b1[None, :], wu, wd, g2[None, :], b2[None, :])


def kernel(x, mem, gdn_w_in, gdn_conv_w, gdn_a_log, gdn_dt_bias, gdn_norm_g, mlstm_w_in, mlstm_b_gate,
           mlstm_norm_g, fox_w_in, fox_b_f, fox_qk_g, mem_w_kv, w_out, ln1_g, ln1_b, w_up, w_down,
           ln2_g, ln2_b):
    bsz, seq, _ = x.shape
    t = bsz * seq
    n_mem = mem.shape[1]
    x = x.reshape(t, D_MODEL)
    xb = x.astype(BF16)
    memb = mem.reshape(bsz * n_mem, D_MODEL).astype(BF16)
    for i in range(DEPTH):
        kind, j = i % N_MIXERS, i // N_MIXERS
        if kind == 0:
            w_in, n_main, n_gate = gdn_w_in[j], 4 * D_MODEL, 2 * GDN_HEADS
        elif kind == 1:
            w_in, n_main, n_gate = mlstm_w_in[j], 3 * D_MODEL, 2 * MLSTM_HEADS
        else:
            w_in, n_main, n_gate = fox_w_in[j], 4 * D_MODEL, FOX_HEADS
        w_gate = w_in[:, n_main:n_main + n_gate]
        w_memq = w_in[:, n_main + n_gate:].astype(BF16)
        proj_in = lambda lo, hi, dt: _matmul(xb, w_in[:, lo:hi].astype(BF16), dt, TM_PROJ, TN_PROJ, "in_proj")
        kv = _matmul(memb, mem_w_kv[i].astype(BF16), BF16, TM_PROJ, TN_PROJ, "mem_kv")
        memo = _mem_attention(xb, w_memq, kv.reshape(bsz, n_mem, 2 * MEM_W), bsz, seq, TM_SMALL)
        wo = w_out[i]
        if kind == 0:
            grow, gsplit = _gates(xb, w_gate, gdn_a_log[j], gdn_dt_bias[j], "gdn", bsz, seq, TM_SMALL)
            qkv = [_proj_conv(xb, w_in[:, n * D_MODEL:(n + 1) * D_MODEL].astype(BF16),
                              gdn_conv_w[j][:, n * D_MODEL:(n + 1) * D_MODEL], seq, TM_PROJ, n < 2,
                              GDN_D ** -0.5 if n == 0 else 1.0) for n in range(3)]
            mix = _gdn_mixer(*qkv, proj_in(GDN_QKV, n_main, BF16), gsplit, grow, gdn_norm_g[j], bsz, seq,
                             TS_MIXER, GDN_GROUP)
        elif kind == 1:
            bias = jnp.concatenate([mlstm_b_gate[j, 0], mlstm_b_gate[j, 1]])
            grow, gsplit = _gates(xb, w_gate, bias, bias, "mlstm", bsz, seq, TM_SMALL)
            mix = _mlstm_mixer(proj_in(0, n_main, BF16), gsplit, grow, mlstm_norm_g[j], bsz, seq, TS_MIXER)
        else:
            order = jnp.argsort(fox_b_f[j])
            cols = (order[:, None] * FOX_DH + jnp.arange(FOX_DH)[None, :]).reshape(-1)
            part = lambda n: w_in[:, n * D_MODEL + cols].astype(BF16)
            crow = _gates(xb, w_gate[:, order], fox_b_f[j][order], fox_b_f[j][order], "fox", bsz, seq, TM_SMALL)
            qn = _proj_headnorm(xb, part(0), jnp.tile(fox_qk_g[j, 0], FOX_HEADS) * (FOX_DH ** -0.5 * LOG2E),
                                TM_PROJ)
            kn = _proj_headnorm(xb, part(1), jnp.tile(fox_qk_g[j, 1], FOX_HEADS), TM_PROJ)
            vog = _matmul(xb, jnp.concatenate([part(2), part(3)], axis=1), BF16, TM_PROJ, TN_PROJ, "in_proj")
            mix = _fox_attention(qn, kn, vog, crow, fox_qk_g[j], bsz, seq, TQ_FOX, FOX_WIDE)
            wo = jnp.concatenate([wo[:D_MODEL][cols], wo[D_MODEL:]], axis=0)
        x, xb = _block_tail(mix.reshape(t, D_MODEL), memo, wo.astype(BF16), x, ln1_g[i], ln1_b[i],
                            w_up[i].astype(BF16), w_down[i].astype(BF16), ln2_g[i], ln2_b[i], TM_TAIL, F_CHUNK)
    return x.reshape(bsz, seq, D_MODEL)
```
